```python
import jax, jax.numpy as jnp
from jax import lax
import numpy as np

D_MODEL = 2048
BATCH = 2
SEQ = 16384
DEPTH = 2

GRID_W = 64
CTX_LEN = 256
GLA_HEADS = 4
GLA_DK = 128
GLA_DV = 256
GATE_RANK = 16
GATE_NORM = 16.0
CHUNK = 64
FN_GROUPS = 4
FN_GROUP_W = 256
MIX_W = GLA_HEADS * GLA_DV + FN_GROUPS * FN_GROUP_W
Q_OFF = 0
K_OFF = Q_OFF + GLA_HEADS * GLA_DK
V_OFF = K_OFF + GLA_HEADS * GLA_DK
R_OFF = V_OFF + GLA_HEADS * GLA_DV
ZF_OFF = R_OFF + GLA_HEADS * GLA_DV
ZB_OFF = ZF_OFF + GATE_RANK
F_OFF = ZB_OFF + GATE_RANK
IN_W = F_OFF + FN_GROUPS * FN_GROUP_W
D_CONV = D_MODEL
CONV_W = 31
D_FF = -(-8 * D_MODEL // (3 * 256)) * 256
EPS = 1e-6

kernel_name = 'hybrid_gla_fnet_conformer_dit_block'


def _rms(x, g):
    xf = x.astype(jnp.float32)
    y = xf * lax.rsqrt(jnp.mean(xf * xf, axis=-1, keepdims=True) + EPS)
    return (y * g.astype(jnp.float32)).astype(x.dtype)


def _layer_norm(x, g, b):
    xf = x.astype(jnp.float32)
    xc = xf - jnp.mean(xf, axis=-1, keepdims=True)
    y = xc * lax.rsqrt(jnp.mean(xc * xc, axis=-1, keepdims=True) + EPS)
    return (y * g.astype(jnp.float32) + b.astype(jnp.float32)).astype(x.dtype)


def _modulate(h, shift, scale):
    return h * (1 + scale) + shift


def _pos_embed_2d(rows, d, dtype):
    r = jnp.repeat(jnp.arange(rows, dtype=jnp.float32), GRID_W)
    col = jnp.tile(jnp.arange(GRID_W, dtype=jnp.float32), rows)
    nq = d // 4
    omega = 1.0 / (10000.0 ** (jnp.arange(nq, dtype=jnp.float32) / nq))
    ar = r[:, None] * omega[None, :]
    ac = col[:, None] * omega[None, :]
    pe = jnp.concatenate([jnp.sin(ar), jnp.cos(ar), jnp.sin(ac), jnp.cos(ac)], axis=-1)
    return pe.astype(dtype)


def _heads(t, dh):
    bsz, n, _ = t.shape
    return t.reshape(bsz, n, -1, dh).transpose(0, 2, 1, 3).astype(jnp.float32)


def _flip(t):
    return jnp.flip(t, axis=2)


def _log_gate(z, w2, b2):
    return jax.nn.log_sigmoid((z @ w2 + b2).astype(jnp.float32)) / GATE_NORM


def _gla_chunked(q, k, v, g, s0):
    bsz, nh, n_tok, dk = q.shape
    dv = v.shape[-1]
    n_chunk = n_tok // CHUNK
    rs = lambda t: t.reshape(bsz, nh, n_chunk, CHUNK, t.shape[-1])
    qc, kc, vc, gc = rs(q), rs(k), rs(v), rs(g)
    b = jnp.cumsum(gc, axis=3)
    b_last = b[:, :, :, -1:, :]
    q_dec = qc * jnp.exp(b)
    k_inv = kc * jnp.exp(-b)
    k_tail = kc * jnp.exp(b_last - b)
    upto = jnp.tril(jnp.ones((CHUNK, CHUNK), dtype=bool))
    att = jnp.where(upto, jnp.einsum('bhnid,bhnjd->bhnij', q_dec, k_inv), 0.0)
    o_intra = jnp.einsum('bhnij,bhnjv->bhniv', att, vc)

    def step(s, inp):
        qd, kt, vv, dec = inp
        o = jnp.einsum('bhid,bhdv->bhiv', qd, s)
        s = dec[..., None] * s + jnp.einsum('bhjd,bhjv->bhdv', kt, vv)
        return s, o

    xs = (jnp.moveaxis(q_dec, 2, 0), jnp.moveaxis(k_tail, 2, 0), jnp.moveaxis(vc, 2, 0),
          jnp.moveaxis(jnp.exp(b_last[:, :, :, 0, :]), 2, 0))
    s_fin, o_inter = lax.scan(step, s0, xs)
    o = o_intra + jnp.moveaxis(o_inter, 0, 2)
    return o.reshape(bsz, nh, n_tok, dv), s_fin


def _gla_final_state(k, v, g):
    cum = jnp.cumsum(g, axis=2)
    w = jnp.exp(cum[:, :, -1:, :] - cum)
    return jnp.einsum('bhld,bhlv->bhdv', k * w, v)


def _bidir_gla(q, k, v, gf, gb, sf0, sb0):
    of, sf = _gla_chunked(q, k, v, gf, sf0)
    ob, sb = _gla_chunked(_flip(q), _flip(k), _flip(v), _flip(gb), sb0)
    return of + _flip(ob), sf, sb


def _fourier(f):
    bsz, n_tok, _ = f.shape
    fg = f.reshape(bsz, n_tok, FN_GROUPS, FN_GROUP_W).astype(jnp.float32)
    y = jnp.fft.fftn(fg, axes=(1, 3), norm='ortho').real
    return y.reshape(bsz, n_tok, FN_GROUPS * FN_GROUP_W).astype(f.dtype)


def _project(a, w_in, gw_f, gb_f, gw_b, gb_b):
    p = a @ w_in
    q = _heads(p[..., Q_OFF:K_OFF], GLA_DK) * (GLA_DK ** -0.5)
    k = _heads(p[..., K_OFF:V_OFF], GLA_DK)
    v = _heads(p[..., V_OFF:R_OFF], GLA_DV)
    gf = _heads(_log_gate(p[..., ZF_OFF:ZB_OFF], gw_f, gb_f), GLA_DK)
    gb = _heads(_log_gate(p[..., ZB_OFF:F_OFF], gw_b, gb_b), GLA_DK)
    return q, k, v, gf, gb, p[..., R_OFF:ZF_OFF], p[..., F_OFF:]


def _mixer_out(o, r, f, norm_g, w_out):
    on = o * lax.rsqrt(jnp.mean(o * o, axis=-1, keepdims=True) + EPS) * norm_g.astype(jnp.float32)
    bsz, nh, n_tok, dv = on.shape
    on = on.transpose(0, 2, 1, 3).reshape(bsz, n_tok, nh * dv).astype(r.dtype) * jax.nn.silu(r)
    return jnp.concatenate([on, _fourier(f)], axis=-1) @ w_out


def _gla_fourier_mixer(a, ac, w_in, gw_f, gb_f, gw_b, gb_b, norm_g, w_out, ctx_out):
    bsz = a.shape[0]
    zero = jnp.zeros((bsz, GLA_HEADS, GLA_DK, GLA_DV), jnp.float32)
    if ctx_out:
        qc, kc, vc, gfc, gbc, rc, fc = _project(ac, w_in, gw_f, gb_f, gw_b, gb_b)
        oc, sf, sb = _bidir_gla(qc, kc, vc, gfc, gbc, zero, zero)
        yc = _mixer_out(oc, rc, fc, norm_g, w_out)
    else:
        kc = _heads(ac @ w_in[:, K_OFF:V_OFF], GLA_DK)
        vc = _heads(ac @ w_in[:, V_OFF:R_OFF], GLA_DV)
        gfc = _heads(_log_gate(ac @ w_in[:, ZF_OFF:ZB_OFF], gw_f, gb_f), GLA_DK)
        gbc = _heads(_log_gate(ac @ w_in[:, ZB_OFF:F_OFF], gw_b, gb_b), GLA_DK)
        sf = _gla_final_state(kc, vc, gfc)
        sb = _gla_final_state(_flip(kc), _flip(vc), _flip(gbc))
        yc = None
    q, k, v, gf, gb, r, f = _project(a, w_in, gw_f, gb_f, gw_b, gb_b)
    o, _, _ = _bidir_gla(q, k, v, gf, gb, sf, sb)
    return _mixer_out(o, r, f, norm_g, w_out), yc


def _conformer_conv(a, w_pw1, b_pw1, w_dw, b_dw, ln_g, ln_b, w_pw2, b_pw2):
    u = a @ w_pw1 + b_pw1
    u = u[..., :D_CONV] * jax.nn.sigmoid(u[..., D_CONV:])
    u = lax.conv_general_dilated(u, w_dw[:, None, :].astype(u.dtype), window_strides=(1,),
                                 padding=[(CONV_W // 2, CONV_W // 2)],
                                 dimension_numbers=('NWC', 'WIO', 'NWC'),
                                 feature_group_count=D_CONV) + b_dw
    u = jax.nn.silu(_layer_norm(u, ln_g, ln_b))
    return u @ w_pw2 + b_pw2


def _swiglu(a, w_in, w_out):
    hg = a @ w_in
    return (jax.nn.silu(hg[..., :D_FF]) * hg[..., D_FF:]) @ w_out


def _ctx_read_later(i):
    return any(j % 2 == 0 for j in range(i + 1, DEPTH))


def setup_inputs(seed: int = 0) -> dict:
    key = jax.random.key(seed)
    ks = iter(jax.random.split(key, 40))
    nrm = lambda shape, scale: jax.random.normal(next(ks), shape, jnp.float32) * scale
    D = D_MODEL
    ne = (DEPTH + 1) // 2
    no = DEPTH // 2
    return {
        'x': nrm((BATCH, SEQ, D), 1.0),
        'c': nrm((BATCH, D), 1.0),
        'ctx': nrm((BATCH, CTX_LEN, D), 1.0),
        'c_ctx': nrm((D,), 1.0),
        'ada_w': nrm((DEPTH, D, 6 * D), 0.5 * D ** -0.5),
        'ada_b': nrm((DEPTH, 6 * D), 0.01),
        'norm_mix_g': 1.0 + nrm((DEPTH, D), 0.05),
        'norm_ffn_g': 1.0 + nrm((DEPTH, D), 0.05),
        'ffn_w_in': nrm((DEPTH, D, 2 * D_FF), D ** -0.5),
        'ffn_w_out': nrm((DEPTH, D_FF, D), D_FF ** -0.5),
        'gm_w_in': nrm((ne, D, IN_W), D ** -0.5),
        'gla_gate_w_fwd': nrm((ne, GATE_RANK, GLA_HEADS * GLA_DK), GATE_RANK ** -0.5),
        'gla_gate_b_fwd': nrm((ne, GLA_HEADS * GLA_DK), 0.1),
        'gla_gate_w_bwd': nrm((ne, GATE_RANK, GLA_HEADS * GLA_DK), GATE_RANK ** -0.5),
        'gla_gate_b_bwd': nrm((ne, GLA_HEADS * GLA_DK), 0.1),
        'gla_norm_g': 1.0 + nrm((ne, GLA_DV), 0.05),
        'gm_w_out': nrm((ne, MIX_W, D), MIX_W ** -0.5),
        'cv_w_pw1': nrm((no, D, 2 * D_CONV), D ** -0.5),
        'cv_b_pw1': nrm((no, 2 * D_CONV), 0.01),
        'cv_w_dw': nrm((no, CONV_W, D_CONV), CONV_W ** -0.5),
        'cv_b_dw': nrm((no, D_CONV), 0.01),
        'cv_ln_g': 1.0 + nrm((no, D_CONV), 0.05),
        'cv_ln_b': nrm((no, D_CONV), 0.01),
        'cv_w_pw2': nrm((no, D_CONV, D), D_CONV ** -0.5),
        'cv_b_pw2': nrm((no, D), 0.01),
        'final_norm_g': 1.0 + nrm((D,), 0.05),
    }


def reference(x, c, ctx, c_ctx, ada_w, ada_b, norm_mix_g, norm_ffn_g, ffn_w_in, ffn_w_out,
              gm_w_in, gla_gate_w_fwd, gla_gate_b_fwd, gla_gate_w_bwd, gla_gate_b_bwd,
              gla_norm_g, gm_w_out, cv_w_pw1, cv_b_pw1, cv_w_dw, cv_b_dw, cv_ln_g, cv_ln_b,
              cv_w_pw2, cv_b_pw2, final_norm_g):
    rows = x.shape[1] // GRID_W
    h = x + _pos_embed_2d(rows, D_MODEL, x.dtype)
    hc = ctx
    silu_c = jax.nn.silu(c)
    silu_cc = jax.nn.silu(c_ctx)
    for i in range(DEPTH):
        j = i // 2
        even = i % 2 == 0
        ctx_next = _ctx_read_later(i)
        m = [t[:, None, :] for t in jnp.split(silu_c @ ada_w[i] + ada_b[i], 6, axis=-1)]
        a = _modulate(_rms(h, norm_mix_g[i]), m[0], m[1])
        if even or ctx_next:
            mc = jnp.split(silu_cc @ ada_w[i] + ada_b[i], 6, axis=-1)
            ac = _modulate(_rms(hc, norm_mix_g[i]), mc[0], mc[1])
        if even:
            y, yc = _gla_fourier_mixer(a, ac, gm_w_in[j], gla_gate_w_fwd[j], gla_gate_b_fwd[j],
                                       gla_gate_w_bwd[j], gla_gate_b_bwd[j], gla_norm_g[j],
                                       gm_w_out[j], ctx_next)
        else:
            cv = (cv_w_pw1[j], cv_b_pw1[j], cv_w_dw[j], cv_b_dw[j], cv_ln_g[j], cv_ln_b[j],
                  cv_w_pw2[j], cv_b_pw2[j])
            y = _conformer_conv(a, *cv)
            yc = _conformer_conv(ac, *cv) if ctx_next else None
        h = h + m[2] * y
        h = h + m[5] * _swiglu(_modulate(_rms(h, norm_ffn_g[i]), m[3], m[4]), ffn_w_in[i], ffn_w_out[i])
        if ctx_next:
            hc = hc + mc[2] * yc
            hc = hc + mc[5] * _swiglu(_modulate(_rms(hc, norm_ffn_g[i]), mc[3], mc[4]),
                                      ffn_w_in[i], ffn_w_out[i])
    return _rms(h, final_norm_g)
```

```python
import functools
import math

import jax
import jax.numpy as jnp
from jax import lax
from jax.experimental import pallas as pl
from jax.experimental.pallas import tpu as pltpu

F32 = jnp.float32
BF16 = jnp.bfloat16

EPS = 1e-6
GRID_W = 64
GLA_HEADS = 4
GLA_DK = 128
GLA_DV = 256
GATE_RANK = 16
GATE_NORM = 16.0
CHUNK = 64
FN_GROUPS = 4
FN_GROUP_W = 256
CONV_W = 31
CONV_HALO = 16
ROWS = 64
Z_PAD = 128

_MIB = 1024 * 1024


def _cparams(sem, vmem_mib):
    return pltpu.CompilerParams(dimension_semantics=sem, vmem_limit_bytes=vmem_mib * _MIB)


def _dot(a, b):
    return jnp.dot(a, b, preferred_element_type=F32)


def _dot_nt(a, b):
    return lax.dot_general(a, b, (((1,), (1,)), ((), ())), preferred_element_type=F32)


def _dot_tn(a, b):
    return lax.dot_general(a, b, (((0,), (0,)), ((), ())), preferred_element_type=F32)


def _silu(x):
    return x * jax.nn.sigmoid(x)


def _ada_kernel(c_ref, w_ref, b_ref, o_ref):
    s = _silu(c_ref[...]).astype(BF16)
    o_ref[...] = _dot(s, w_ref[...].astype(BF16)) + b_ref[...]


def _ada_call(cin, ada_w, ada_b):
    depth, d, n = ada_w.shape
    tn = 1024
    return pl.pallas_call(
        _ada_kernel,
        grid=(depth, n // tn),
        in_specs=[pl.BlockSpec((8, d), lambda l, j: (0, 0)),
                  pl.BlockSpec((None, d, tn), lambda l, j: (l, 0, j)),
                  pl.BlockSpec((None, 1, tn), lambda l, j: (l, 0, j))],
        out_specs=pl.BlockSpec((None, 8, tn), lambda l, j: (l, 0, j)),
        out_shape=jax.ShapeDtypeStruct((depth, 8, n), F32),
        compiler_params=_cparams(("parallel", "parallel"), 40),
        name="ada_modulation",
    )(cin, ada_w, ada_b.reshape(depth, 1, n))


def _split_bf16(v):
    hi = v.astype(BF16)
    lo = (v - hi.astype(F32)).astype(BF16)
    return hi, lo


def _fold_dft_kernel(wf_ref, cs_ref, zr_ref, zi_ref):
    w_hi, w_lo = _split_bf16(wf_ref[...])
    c_hi, c_lo = _split_bf16(cs_ref[...])
    r = _dot(w_hi, c_hi) + _dot(w_hi, c_lo) + _dot(w_lo, c_hi)
    zr_ref[...] = r[:, :FN_GROUP_W].astype(BF16)
    zi_ref[...] = r[:, FN_GROUP_W:].astype(BF16)


def _fold_dft_call(wf, cs):
    d = wf.shape[0]
    gw = FN_GROUP_W
    out = jax.ShapeDtypeStruct((d, FN_GROUPS * gw), BF16)
    return pl.pallas_call(
        _fold_dft_kernel,
        grid=(FN_GROUPS,),
        in_specs=[pl.BlockSpec((d, gw), lambda g: (0, g)),
                  pl.BlockSpec((gw, 2 * gw), lambda g: (0, 0))],
        out_specs=[pl.BlockSpec((d, gw), lambda g: (0, g)),
                   pl.BlockSpec((d, gw), lambda g: (0, g))],
        out_shape=[out, out],
        compiler_params=_cparams(("parallel",), 32),
        name="fold_channel_dft",
    )(wf, cs)


def _rms_mod_prologue(x_ref, pe_refs, gs_ref, sh_ref, a_sc, tm):
    gs = gs_ref[...]
    sh = sh_ref[...]
    half = x_ref.shape[-1] // 2

    def body(r, carry):
        r0 = pl.multiple_of(r * ROWS, ROWS)
        h = x_ref[pl.ds(r0, ROWS), :]
        if pe_refs is not None:
            per_ref, pec_ref = pe_refs
            row = jnp.broadcast_to(per_ref[pl.ds(r, 1), :], (ROWS, half))
            h = h + jnp.concatenate([row, pec_ref[...]], axis=-1)
        ms = jnp.mean(h * h, axis=-1, keepdims=True)
        a_sc[pl.ds(r0, ROWS), :] = (h * lax.rsqrt(ms + EPS) * gs + sh).astype(BF16)
        return carry

    lax.fori_loop(0, tm // ROWS, body, 0)


def _inproj_kernel(*refs, tm, with_pe):
    if with_pe:
        x_ref, per_ref, pec_ref, gs_ref, sh_ref, w_ref, wz_ref, p_ref, z_ref, a_sc = refs
        pe_refs = (per_ref, pec_ref)
    else:
        x_ref, gs_ref, sh_ref, w_ref, wz_ref, p_ref, z_ref, a_sc = refs
        pe_refs = None

    @pl.when(pl.program_id(2) == 0)
    def _():
        _rms_mod_prologue(x_ref, pe_refs, gs_ref, sh_ref, a_sc, tm)
        z_ref[...] = _dot(a_sc[...], wz_ref[...])

    p_ref[...] = _dot(a_sc[...], w_ref[...]).astype(BF16)


def _inproj_call(x, pe, gs, sh, w, wz, tm, tn):
    b, l, d = x.shape
    n = w.shape[1]
    with_pe = pe is not None
    in_specs = [pl.BlockSpec((None, tm, d), lambda bi, i, j: (bi, i, 0))]
    args = [x]
    if with_pe:
        per, pec = pe
        in_specs += [pl.BlockSpec((tm // GRID_W, d // 2), lambda bi, i, j: (i, 0)),
                     pl.BlockSpec((GRID_W, d // 2), lambda bi, i, j: (0, 0))]
        args += [per, pec]
    in_specs += [pl.BlockSpec((None, 1, d), lambda bi, i, j: (bi, 0, 0)),
                 pl.BlockSpec((None, 1, d), lambda bi, i, j: (bi, 0, 0)),
                 pl.BlockSpec((d, tn), lambda bi, i, j: (0, j)),
                 pl.BlockSpec((d, Z_PAD), lambda bi, i, j: (0, 0))]
    args += [gs, sh, w, wz]
    return pl.pallas_call(
        functools.partial(_inproj_kernel, tm=tm, with_pe=with_pe),
        grid=(b, l // tm, n // tn),
        in_specs=in_specs,
        out_specs=[pl.BlockSpec((None, tm, tn), lambda bi, i, j: (bi, i, j)),
                   pl.BlockSpec((None, tm, Z_PAD), lambda bi, i, j: (bi, i, 0))],
        out_shape=[jax.ShapeDtypeStruct((b, l, n), BF16), jax.ShapeDtypeStruct((b, l, Z_PAD), F32)],
        scratch_shapes=[pltpu.VMEM((tm, d), BF16)],
        compiler_params=_cparams(("parallel", "parallel", "arbitrary"), 48),
        name="input_projection_pe" if with_pe else "input_projection_ctx",
    )(*args)


def _tri_mask(n, fwd):
    row = lax.broadcasted_iota(jnp.int32, (n, n), 0)
    col = lax.broadcasted_iota(jnp.int32, (n, n), 1)
    return (col <= row) if fwd else (col >= row)


def _gate_cum(z_ref, gw_ref, gb_ref, mask):
    logit = _dot(z_ref[...].astype(BF16), gw_ref[...]) + gb_ref[...]
    g = (jnp.minimum(logit, 0.0) - jnp.log1p(jnp.exp(-jnp.abs(logit)))) * (1.0 / GATE_NORM)
    tri = jnp.where(mask, 1.0, 0.0).astype(BF16)
    g_hi, g_lo = _split_bf16(g)
    return _dot(tri, g_hi) + _dot(tri, g_lo)


def _gla_kernel(*refs, fwd, n):
    if fwd:
        (q_ref, k_ref, v_ref, z_ref, gw_ref, gb_ref, s0_ref, ob_ref, r_ref, ng_ref, o_ref, st_sc) = refs
    else:
        (q_ref, k_ref, v_ref, z_ref, gw_ref, gb_ref, s0_ref, o_ref, st_sc) = refs

    @pl.when(pl.program_id(1) == 0)
    def _():
        st_sc[...] = s0_ref[...]

    mask = _tri_mask(n, fwd)
    b = _gate_cum(z_ref, gw_ref, gb_ref, mask)
    last = n - 1 if fwd else 0
    scale = GLA_DK ** -0.5
    for h in range(GLA_HEADS):
        ks = slice(h * GLA_DK, (h + 1) * GLA_DK)
        vs = slice(h * GLA_DV, (h + 1) * GLA_DV)
        bh = b[:, ks]
        bl = bh[last:last + 1, :]
        q = q_ref[:, ks].astype(F32) * scale
        k = k_ref[:, ks].astype(F32)
        v = v_ref[:, vs]
        qd = (q * jnp.exp(bh)).astype(BF16)
        ki = (k * jnp.exp(-bh)).astype(BF16)
        kt = (k * jnp.exp(bl - bh)).astype(BF16)
        att = jnp.where(mask, _dot_nt(qd, ki), 0.0).astype(BF16)
        st = st_sc[h]
        o = _dot(att, v) + _dot_nt(qd, st.astype(BF16))
        st_sc[h] = jnp.exp(bl) * st + _dot_tn(v, kt)
        if fwd:
            o = o + ob_ref[:, vs]
            on = o * lax.rsqrt(jnp.mean(o * o, axis=-1, keepdims=True) + EPS) * ng_ref[...]
            o_ref[:, vs] = (on * _silu(r_ref[:, vs].astype(F32))).astype(BF16)
        else:
            o_ref[:, vs] = o


def _gla_call(p, z, gw, gb, s0, fwd, ob=None, ng=None):
    b, l, _ = p.shape
    n = CHUNK
    nc = l // n
    hk = GLA_HEADS * GLA_DK
    hv = GLA_HEADS * GLA_DV
    cidx = (lambda c: c) if fwd else (lambda c: nc - 1 - c)
    in_specs = [pl.BlockSpec((None, n, hk), lambda bi, c: (bi, cidx(c), 0)),
                pl.BlockSpec((None, n, hk), lambda bi, c: (bi, cidx(c), 1)),
                pl.BlockSpec((None, n, hv), lambda bi, c: (bi, cidx(c), 1)),
                pl.BlockSpec((None, n, Z_PAD), lambda bi, c: (bi, cidx(c), 0)),
                pl.BlockSpec((Z_PAD, hk), lambda bi, c: (0, 0)),
                pl.BlockSpec((1, hk), lambda bi, c: (0, 0)),
                pl.BlockSpec((None, GLA_HEADS, GLA_DV, GLA_DK), lambda bi, c: (bi, 0, 0, 0))]
    args = [p, p, p, z, gw, gb, s0]
    if fwd:
        in_specs += [pl.BlockSpec((None, n, hv), lambda bi, c: (bi, cidx(c), 0)),
                     pl.BlockSpec((None, n, hv), lambda bi, c: (bi, cidx(c), 2)),
                     pl.BlockSpec((1, GLA_DV), lambda bi, c: (0, 0))]
        args += [ob, p, ng]
    return pl.pallas_call(
        functools.partial(_gla_kernel, fwd=fwd, n=n),
        grid=(b, nc),
        in_specs=in_specs,
        out_specs=pl.BlockSpec((None, n, hv), lambda bi, c: (bi, cidx(c), 0)),
        out_shape=jax.ShapeDtypeStruct((b, l, hv), BF16 if fwd else F32),
        scratch_shapes=[pltpu.VMEM((GLA_HEADS, GLA_DV, GLA_DK), F32)],
        compiler_params=_cparams(("parallel", "arbitrary"), 32),
        name="gla_scan_fwd" if fwd else "gla_scan_bwd",
    )(*args)


def _ctx_state_kernel(k_ref, v_ref, z_ref, gw_ref, gb_ref, s_ref, *, fwd, n):
    mask = _tri_mask(n, fwd)
    b = _gate_cum(z_ref, gw_ref, gb_ref, mask)
    last = n - 1 if fwd else 0
    for h in range(GLA_HEADS):
        ks = slice(h * GLA_DK, (h + 1) * GLA_DK)
        vs = slice(h * GLA_DV, (h + 1) * GLA_DV)
        bh = b[:, ks]
        kt = (k_ref[:, ks].astype(F32) * jnp.exp(bh[last:last + 1, :] - bh)).astype(BF16)
        s_ref[h] = _dot_tn(v_ref[:, vs], kt)


def _ctx_state_call(pc, zc, gw, gb, fwd):
    b, n, _ = pc.shape
    hk = GLA_HEADS * GLA_DK
    hv = GLA_HEADS * GLA_DV
    return pl.pallas_call(
        functools.partial(_ctx_state_kernel, fwd=fwd, n=n),
        grid=(b,),
        in_specs=[pl.BlockSpec((None, n, hk), lambda bi: (bi, 0, 1)),
                  pl.BlockSpec((None, n, hv), lambda bi: (bi, 0, 1)),
                  pl.BlockSpec((None, n, Z_PAD), lambda bi: (bi, 0, 0)),
                  pl.BlockSpec((Z_PAD, hk), lambda bi: (0, 0)),
                  pl.BlockSpec((1, hk), lambda bi: (0, 0))],
        out_specs=pl.BlockSpec((None, GLA_HEADS, GLA_DV, GLA_DK), lambda bi: (bi, 0, 0, 0)),
        out_shape=jax.ShapeDtypeStruct((b, GLA_HEADS, GLA_DV, GLA_DK), F32),
        compiler_params=_cparams(("parallel",), 32),
        name="ctx_state_fwd" if fwd else "ctx_state_bwd",
    )(pc, pc, zc, gw, gb)


def _fft1_kernel(zr_ref, zi_ref, w_ref, o_ref, *, n1):
    zz = jnp.concatenate([zr_ref[...], zi_ref[...]], axis=0)
    a = _dot(w_ref[...], zz)
    o_ref[0] = a[:n1].astype(BF16)
    o_ref[1] = a[n1:].astype(BF16)


def _fft2_kernel(a_ref, m_ref, o_ref):
    aa = jnp.concatenate([a_ref[0], a_ref[1]], axis=0)
    o_ref[...] = _dot(m_ref[...], aa).astype(BF16)


def _fourier_call(p, w1, m2, n1, n2, zr_blk, zi_blk):
    b, l, npc = p.shape
    cw = FN_GROUPS * FN_GROUP_W
    nblk = npc // cw
    p2 = p.reshape(b, n1, n2 * npc)
    a = pl.pallas_call(
        functools.partial(_fft1_kernel, n1=n1),
        grid=(b, n2),
        in_specs=[pl.BlockSpec((None, n1, cw), lambda bi, s: (bi, 0, s * nblk + zr_blk)),
                  pl.BlockSpec((None, n1, cw), lambda bi, s: (bi, 0, s * nblk + zi_blk)),
                  pl.BlockSpec((2 * n1, 2 * n1), lambda bi, s: (0, 0))],
        out_specs=pl.BlockSpec((None, 2, n1, cw), lambda bi, s: (bi, 0, 0, s)),
        out_shape=jax.ShapeDtypeStruct((b, 2, n1, n2 * cw), BF16),
        compiler_params=_cparams(("parallel", "parallel"), 32),
        name="position_dft_stage1",
    )(p2, p2, w1)
    a = a.reshape(b, 2, n1 * n2, cw)
    y = pl.pallas_call(
        _fft2_kernel,
        grid=(b, n1),
        in_specs=[pl.BlockSpec((None, 2, n2, cw), lambda bi, k1: (bi, 0, k1, 0)),
                  pl.BlockSpec((None, n2, 2 * n2), lambda bi, k1: (k1, 0, 0))],
        out_specs=pl.BlockSpec((None, n2, cw), lambda bi, k1: (bi, 0, k1)),
        out_shape=jax.ShapeDtypeStruct((b, n2, n1 * cw), BF16),
        compiler_params=_cparams(("parallel", "parallel"), 32),
        name="position_dft_stage2",
    )(a, m2)
    return y.reshape(b, l, cw)


def _outproj_kernel(x1_ref, x2_ref, w1_ref, w2_ref, xr_ref, per_ref, pec_ref, gate_ref, o_ref, *, tm):
    y = _dot(x1_ref[...], w1_ref[...]) + _dot(x2_ref[...], w2_ref[...])
    first = pl.program_id(2) == 0
    gate = gate_ref[...]
    half = o_ref.shape[-1]
    for r in range(tm // ROWS):
        rows = slice(r * ROWS, (r + 1) * ROWS)
        pe = jnp.where(first, jnp.broadcast_to(per_ref[r:r + 1, :], (ROWS, half)), pec_ref[...])
        o_ref[rows, :] = xr_ref[rows, :] + pe + gate * y[rows, :]


def _outproj_call(x1, x2, w, x, per, pec, gate, tm):
    b, l, d = x.shape
    kh = x1.shape[-1]
    half = d // 2
    return pl.pallas_call(
        functools.partial(_outproj_kernel, tm=tm),
        grid=(b, l // tm, 2),
        in_specs=[pl.BlockSpec((None, tm, kh), lambda bi, i, j: (bi, i, 0)),
                  pl.BlockSpec((None, tm, kh), lambda bi, i, j: (bi, i, 0)),
                  pl.BlockSpec((kh, half), lambda bi, i, j: (0, j)),
                  pl.BlockSpec((kh, half), lambda bi, i, j: (1, j)),
                  pl.BlockSpec((None, tm, half), lambda bi, i, j: (bi, i, j)),
                  pl.BlockSpec((tm // GRID_W, half), lambda bi, i, j: (i, 0)),
                  pl.BlockSpec((GRID_W, half), lambda bi, i, j: (0, 0)),
                  pl.BlockSpec((None, 1, half), lambda bi, i, j: (bi, 0, j))],
        out_specs=pl.BlockSpec((None, tm, half), lambda bi, i, j: (bi, i, j)),
        out_shape=jax.ShapeDtypeStruct((b, l, d), F32),
        compiler_params=_cparams(("parallel", "parallel", "arbitrary"), 48),
        name="output_projection",
    )(x1, x2, w, w, x, per, pec, gate)


def _pair_kernel(*refs, tm, glu):
    if glu:
        x_ref, gs_ref, sh_ref, wa_ref, wb_ref, ba_ref, bb_ref, o_ref, a_sc = refs
    else:
        x_ref, gs_ref, sh_ref, wa_ref, wb_ref, o_ref, a_sc = refs

    @pl.when(pl.program_id(2) == 0)
    def _():
        _rms_mod_prologue(x_ref, None, gs_ref, sh_ref, a_sc, tm)

    a = a_sc[...]
    ya = _dot(a, wa_ref[...])
    yb = _dot(a, wb_ref[...])
    if glu:
        o_ref[...] = ((ya + ba_ref[...]) * jax.nn.sigmoid(yb + bb_ref[...])).astype(BF16)
    else:
        o_ref[...] = (_silu(ya) * yb).astype(BF16)


def _pair_call(x, gs, sh, w, bias, tm, tn, name):
    b, l, d = x.shape
    nh = w.shape[1] // 2
    nj = nh // tn
    glu = bias is not None
    in_specs = [pl.BlockSpec((None, tm, d), lambda bi, i, j: (bi, i, 0)),
                pl.BlockSpec((None, 1, d), lambda bi, i, j: (bi, 0, 0)),
                pl.BlockSpec((None, 1, d), lambda bi, i, j: (bi, 0, 0)),
                pl.BlockSpec((d, tn), lambda bi, i, j: (0, j)),
                pl.BlockSpec((d, tn), lambda bi, i, j: (0, j + nj))]
    args = [x, gs, sh, w, w]
    if glu:
        in_specs += [pl.BlockSpec((1, tn), lambda bi, i, j: (0, j)),
                     pl.BlockSpec((1, tn), lambda bi, i, j: (0, j + nj))]
        args += [bias, bias]
    return pl.pallas_call(
        functools.partial(_pair_kernel, tm=tm, glu=glu),
        grid=(b, l // tm, nj),
        in_specs=in_specs,
        out_specs=pl.BlockSpec((None, tm, tn), lambda bi, i, j: (bi, i, j)),
        out_shape=jax.ShapeDtypeStruct((b, l, nh), BF16),
        scratch_shapes=[pltpu.VMEM((tm, d), BF16)],
        compiler_params=_cparams(("parallel", "parallel", "arbitrary"), 48),
        name=name,
    )(*args)


def _ffn_out_kernel(x_ref, w_ref, res_ref, gate_ref, o_ref):
    o_ref[...] = res_ref[...] + gate_ref[...] * _dot(x_ref[...], w_ref[...])


def _ffn_out_call(xh, w, res, gate, tm, tn):
    b, l, kk = xh.shape
    d = w.shape[1]
    return pl.pallas_call(
        _ffn_out_kernel,
        grid=(b, l // tm, d // tn),
        in_specs=[pl.BlockSpec((None, tm, kk), lambda bi, i, j: (bi, i, 0)),
                  pl.BlockSpec((kk, tn), lambda bi, i, j: (0, j)),
                  pl.BlockSpec((None, tm, tn), lambda bi, i, j: (bi, i, j)),
                  pl.BlockSpec((None, 1, tn), lambda bi, i, j: (bi, 0, j))],
        out_specs=pl.BlockSpec((None, tm, tn), lambda bi, i, j: (bi, i, j)),
        out_shape=jax.ShapeDtypeStruct((b, l, d), F32),
        compiler_params=_cparams(("parallel", "parallel", "arbitrary"), 48),
        name="ffn_output_projection",
    )(xh, w, res, gate)


def _conv_ln_prologue(um_ref, up_ref, un_ref, wdw_ref, bdw_ref, lng_ref, lnb_ref, ext_sc, cv_sc, a_sc, tm):
    i = pl.program_id(1)
    d = um_ref.shape[-1]
    halo = CONV_HALO
    ext_sc[0:halo, :] = jnp.where(i > 0, up_ref[...].astype(F32), 0.0)
    ext_sc[halo:halo + tm, :] = um_ref[...].astype(F32)
    ext_sc[halo + tm:, :] = jnp.where(i < pl.num_programs(1) - 1, un_ref[...].astype(F32), 0.0)
    n_a = (CONV_W + 8) // 8
    span = ROWS + 8
    lanes = 128

    def body(r, carry):
        r0 = pl.multiple_of(r * ROWS, ROWS)
        for cb in range(d // lanes):
            cs = slice(cb * lanes, (cb + 1) * lanes)
            e = ext_sc[pl.ds(r0, span + 8 * (n_a - 1)), cs]
            acc = None
            for s in range(8):
                g = None
                for a in range(n_a):
                    m = 8 * a + s
                    if m == 0 or m > CONV_W:
                        continue
                    term = wdw_ref[m:m + 1, cs] * e[8 * a:8 * a + span, :]
                    g = term if g is None else g + term
                gs = g if s == 0 else pltpu.roll(g, span - s, 0)
                acc = gs[:ROWS, :] if acc is None else acc + gs[:ROWS, :]
            cv_sc[pl.ds(r0, ROWS), cs] = acc + bdw_ref[:, cs]
        c = cv_sc[pl.ds(r0, ROWS), :]
        xc = c - jnp.mean(c, axis=-1, keepdims=True)
        y = xc * lax.rsqrt(jnp.mean(xc * xc, axis=-1, keepdims=True) + EPS) * lng_ref[...] + lnb_ref[...]
        a_sc[pl.ds(r0, ROWS), :] = _silu(y).astype(BF16)
        return carry

    lax.fori_loop(0, tm // ROWS, body, 0)


def _conv_pw2_kernel(um_ref, up_ref, un_ref, wdw_ref, bdw_ref, lng_ref, lnb_ref, w_ref, b2_ref, res_ref, gate_ref,
                     o_ref, ext_sc, cv_sc, a_sc, *, tm):
    @pl.when(pl.program_id(2) == 0)
    def _():
        _conv_ln_prologue(um_ref, up_ref, un_ref, wdw_ref, bdw_ref, lng_ref, lnb_ref, ext_sc, cv_sc, a_sc, tm)

    o_ref[...] = res_ref[...] + gate_ref[...] * (_dot(a_sc[...], w_ref[...]) + b2_ref[...])


def _conv_pw2_call(u, wdw, bdw, lng, lnb, w, b2, res, gate, tm, tn):
    b, l, d = u.shape
    halo = CONV_HALO
    hb = tm // halo
    nhb = l // halo
    return pl.pallas_call(
        functools.partial(_conv_pw2_kernel, tm=tm),
        grid=(b, l // tm, d // tn),
        in_specs=[pl.BlockSpec((None, tm, d), lambda bi, i, j: (bi, i, 0)),
                  pl.BlockSpec((None, halo, d), lambda bi, i, j: (bi, jnp.maximum(i * hb - 1, 0), 0)),
                  pl.BlockSpec((None, halo, d), lambda bi, i, j: (bi, jnp.minimum((i + 1) * hb, nhb - 1), 0)),
                  pl.BlockSpec((CONV_W + 1, d), lambda bi, i, j: (0, 0)),
                  pl.BlockSpec((1, d), lambda bi, i, j: (0, 0)),
                  pl.BlockSpec((1, d), lambda bi, i, j: (0, 0)),
                  pl.BlockSpec((1, d), lambda bi, i, j: (0, 0)),
                  pl.BlockSpec((d, tn), lambda bi, i, j: (0, j)),
                  pl.BlockSpec((1, tn), lambda bi, i, j: (0, j)),
                  pl.BlockSpec((None, tm, tn), lambda bi, i, j: (bi, i, j)),
                  pl.BlockSpec((None, 1, tn), lambda bi, i, j: (bi, 0, j))],
        out_specs=pl.BlockSpec((None, tm, tn), lambda bi, i, j: (bi, i, j)),
        out_shape=jax.ShapeDtypeStruct((b, l, d), F32),
        scratch_shapes=[pltpu.VMEM((tm + 2 * halo, d), F32),
                        pltpu.VMEM((tm, d), F32),
                        pltpu.VMEM((tm, d), BF16)],
        compiler_params=_cparams(("parallel", "parallel", "arbitrary"), 48),
        name="conv_ln_pointwise2",
    )(u, u, u, wdw, bdw, lng, lnb, w, b2, res, gate)


def _final_norm_kernel(x_ref, g_ref, o_ref, *, tm):
    g = g_ref[...]

    def body(r, carry):
        r0 = pl.multiple_of(r * ROWS, ROWS)
        h = x_ref[pl.ds(r0, ROWS), :]
        o_ref[pl.ds(r0, ROWS), :] = h * lax.rsqrt(jnp.mean(h * h, axis=-1, keepdims=True) + EPS) * g
        return carry

    lax.fori_loop(0, tm // ROWS, body, 0)


def _final_norm_call(x, g, tm):
    b, l, d = x.shape
    return pl.pallas_call(
        functools.partial(_final_norm_kernel, tm=tm),
        grid=(b, l // tm),
        in_specs=[pl.BlockSpec((None, tm, d), lambda bi, i: (bi, i, 0)),
                  pl.BlockSpec((1, d), lambda bi, i: (0, 0))],
        out_specs=pl.BlockSpec((None, tm, d), lambda bi, i: (bi, i, 0)),
        out_shape=jax.ShapeDtypeStruct((b, l, d), F32),
        compiler_params=_cparams(("parallel", "parallel"), 40),
        name="final_rmsnorm",
    )(x, g)


def _pos_tables(rows, d):
    nq = d // 4
    omega = 1.0 / (10000.0 ** (jnp.arange(nq, dtype=F32) / nq))
    ar = jnp.arange(rows, dtype=F32)[:, None] * omega[None, :]
    ac = jnp.arange(GRID_W, dtype=F32)[:, None] * omega[None, :]
    per = jnp.concatenate([jnp.sin(ar), jnp.cos(ar)], axis=-1)
    pec = jnp.concatenate([jnp.sin(ac), jnp.cos(ac)], axis=-1)
    return per, pec


def _cos_sin(num, den):
    ang = (num % den).astype(F32) * (2.0 * math.pi / den)
    return jnp.cos(ang), jnp.sin(ang)


def _dft_tables(n1, n2):
    i32 = jnp.int32
    gw = FN_GROUP_W
    cc, sc = _cos_sin(jnp.arange(gw, dtype=i32)[:, None] * jnp.arange(gw, dtype=i32)[None, :], gw)
    chan = jnp.concatenate([cc, -sc], axis=-1)
    c1, s1 = _cos_sin(jnp.arange(n1, dtype=i32)[:, None] * jnp.arange(n1, dtype=i32)[None, :], n1)
    w1 = jnp.concatenate([jnp.concatenate([c1, s1], axis=-1),
                          jnp.concatenate([-s1, c1], axis=-1)], axis=0).astype(BF16)
    k1 = jnp.arange(n1, dtype=i32)[:, None, None]
    k2 = jnp.arange(n2, dtype=i32)[None, :, None]
    nn = jnp.arange(n2, dtype=i32)[None, None, :]
    c2, s2 = _cos_sin(nn * (k1 + n1 * k2), n1 * n2)
    norm = 1.0 / math.sqrt(n1 * n2 * gw)
    m2 = (jnp.concatenate([c2, s2], axis=-1) * norm).astype(BF16)
    return chan, w1, m2


def _split_len(l):
    n1 = 1 << ((l.bit_length() - 1) // 2)
    n2 = l // n1
    assert n1 * n2 == l and n1 % 8 == 0 and n2 % 16 == 0, l
    return n1, n2


def kernel(x, c, ctx, c_ctx, ada_w, ada_b, norm_mix_g, norm_ffn_g, ffn_w_in, ffn_w_out, gm_w_in, gla_gate_w_fwd, gla_gate_b_fwd, gla_gate_w_bwd, gla_gate_b_bwd, gla_norm_g, gm_w_out, cv_w_pw1, cv_b_pw1, cv_w_dw, cv_b_dw, cv_ln_g, cv_ln_b, cv_w_pw2, cv_b_pw2, final_norm_g):
    b, l, d = x.shape
    depth = ada_w.shape[0]
    assert depth == 2 and b + 1 <= 8
    hk = GLA_HEADS * GLA_DK
    hv = GLA_HEADS * GLA_DV
    cw = FN_GROUPS * FN_GROUP_W
    n1, n2 = _split_len(l)
    tm_big = min(1024, l)
    tm_mid = min(512, l)

    per, pec = _pos_tables(l // GRID_W, d)
    chan, w1, m2 = _dft_tables(n1, n2)

    cin = jnp.zeros((8, d), F32).at[:b].set(c).at[b].set(c_ctx)
    mod = _ada_call(cin, ada_w, ada_b).reshape(depth, 8, 6, d)

    def mod_vecs(layer, rows, g, which):
        shift, scale = mod[layer, rows, 2 * which + 0 + which], mod[layer, rows, 2 * which + 1 + which]
        return (g[None, :] * (1.0 + scale))[:, None, :], shift[:, None, :]

    def gate_vec(layer, rows, which):
        return mod[layer, rows, 3 * which + 2][:, None, :]

    lat = slice(0, b)
    w_in = gm_w_in[0]
    q_off, k_off, v_off, r_off = 0, hk, 2 * hk, 2 * hk + hv
    zf_off = r_off + hv
    f_off = zf_off + 2 * GATE_RANK
    wzr, wzi = _fold_dft_call(w_in[:, f_off:], chan)
    w_main = jnp.concatenate([w_in[:, :zf_off].astype(BF16), wzr, wzi], axis=-1)
    w_z = jnp.pad(w_in[:, zf_off:f_off], ((0, 0), (0, Z_PAD - 2 * GATE_RANK))).astype(BF16)

    def gate_w(gw, off):
        return jnp.pad(gw, ((off, Z_PAD - GATE_RANK - off), (0, 0))).astype(BF16)

    gw_f, gw_b = gate_w(gla_gate_w_fwd[0], 0), gate_w(gla_gate_w_bwd[0], GATE_RANK)
    gb_f, gb_b = gla_gate_b_fwd[0][None, :], gla_gate_b_bwd[0][None, :]

    gs, sh = mod_vecs(0, lat, norm_mix_g[0], 0)
    gs_c, sh_c = mod_vecs(0, slice(b, b + 1), norm_mix_g[0], 0)
    gs_c, sh_c = jnp.broadcast_to(gs_c, (b, 1, d)), jnp.broadcast_to(sh_c, (b, 1, d))

    tn = 512
    pc, zc = _inproj_call(ctx, None, gs_c, sh_c, w_main, w_z, ctx.shape[1], tn)
    p, z = _inproj_call(x, (per, pec), gs, sh, w_main, w_z, tm_big, tn)

    s0_f = _ctx_state_call(pc, zc, gw_f, gb_f, True)
    s0_b = _ctx_state_call(pc, zc, gw_b, gb_b, False)
    ob = _gla_call(p, z, gw_b, gb_b, s0_b, False)
    og = _gla_call(p, z, gw_f, gb_f, s0_f, True, ob=ob, ng=gla_norm_g[0][None, :])

    zr_blk = (r_off + hv) // cw
    yf = _fourier_call(p, w1, m2, n1, n2, zr_blk, zr_blk + 1)

    h = _outproj_call(og, yf, gm_w_out[0].astype(BF16), x, per, pec, gate_vec(0, lat, 0), tm_big)

    def ffn(hin, layer):
        gs2, sh2 = mod_vecs(layer, lat, norm_ffn_g[layer], 1)
        hid = _pair_call(hin, gs2, sh2, ffn_w_in[layer].astype(BF16), None, tm_big, 512, "ffn_swiglu_in")
        return _ffn_out_call(hid, ffn_w_out[layer].astype(BF16), hin, gate_vec(layer, lat, 1), tm_mid, 512)

    h = ffn(h, 0)

    gs, sh = mod_vecs(1, lat, norm_mix_g[1], 0)
    u = _pair_call(h, gs, sh, cv_w_pw1[0].astype(BF16), cv_b_pw1[0][None, :], tm_big, 512, "pointwise1_glu")
    wdw = jnp.pad(cv_w_dw[0], ((1, 0), (0, 0)))
    h = _conv_pw2_call(u, wdw, cv_b_dw[0][None, :], cv_ln_g[0][None, :], cv_ln_b[0][None, :],
                       cv_w_pw2[0].astype(BF16), cv_b_pw2[0][None, :], h, gate_vec(1, lat, 0), tm_mid, 1024)
    h = ffn(h, 1)

    return _final_norm_call(h, final_norm_g[None, :], tm_mid)
```

```python
import functools
import math

import jax
import jax.numpy as jnp
from jax import lax
from jax.experimental import pallas as pl
from jax.experimental.pallas import tpu as pltpu

F32 = jnp.float32
BF16 = jnp.bfloat16

EPS = 1e-6
GRID_W = 64
GLA_HEADS = 4
GLA_DK = 128
GLA_DV = 256
GATE_RANK = 16
GATE_NORM = 16.0
CHUNK = 64
GLA_SUBCHUNKS = 2
FN_GROUPS = 4
FN_GROUP_W = 256
FFT_COLS = 8
CONV_W = 31
CONV_HALO = 16
ROWS = 64
Z_PAD = 128

_MIB = 1024 * 1024


def _cparams(sem, vmem_mib):
    return pltpu.CompilerParams(dimension_semantics=sem, vmem_limit_bytes=vmem_mib * _MIB)


def _dot(a, b):
    return jnp.dot(a, b, preferred_element_type=F32)


def _dot_nt(a, b):
    return lax.dot_general(a, b, (((1,), (1,)), ((), ())), preferred_element_type=F32)


def _dot_tn(a, b):
    return lax.dot_general(a, b, (((0,), (0,)), ((), ())), preferred_element_type=F32)


def _silu(x):
    return x * jax.nn.sigmoid(x)


def _ada_kernel(c_ref, w_ref, b_ref, o_ref):
    s = _silu(c_ref[...]).astype(BF16)
    o_ref[...] = _dot(s, w_ref[...].astype(BF16)) + b_ref[...]


def _ada_call(cin, ada_w, ada_b):
    depth, d, n = ada_w.shape
    tn = 1024
    return pl.pallas_call(
        _ada_kernel,
        grid=(depth, n // tn),
        in_specs=[pl.BlockSpec((8, d), lambda l, j: (0, 0)),
                  pl.BlockSpec((None, d, tn), lambda l, j: (l, 0, j)),
                  pl.BlockSpec((None, 1, tn), lambda l, j: (l, 0, j))],
        out_specs=pl.BlockSpec((None, 8, tn), lambda l, j: (l, 0, j)),
        out_shape=jax.ShapeDtypeStruct((depth, 8, n), F32),
        compiler_params=_cparams(("parallel", "parallel"), 40),
        name="ada_modulation",
    )(cin, ada_w, ada_b.reshape(depth, 1, n))


def _split_bf16(v):
    hi = v.astype(BF16)
    lo = (v - hi.astype(F32)).astype(BF16)
    return hi, lo


def _fold_dft_kernel(wf_ref, cs_ref, zr_ref, zi_ref):
    w_hi, w_lo = _split_bf16(wf_ref[...])
    c_hi, c_lo = _split_bf16(cs_ref[...])
    r = _dot(w_hi, c_hi) + _dot(w_hi, c_lo) + _dot(w_lo, c_hi)
    zr_ref[...] = r[:, :FN_GROUP_W].astype(BF16)
    zi_ref[...] = r[:, FN_GROUP_W:].astype(BF16)


def _fold_dft_call(wf, cs):
    d = wf.shape[0]
    gw = FN_GROUP_W
    out = jax.ShapeDtypeStruct((d, FN_GROUPS * gw), BF16)
    return pl.pallas_call(
        _fold_dft_kernel,
        grid=(FN_GROUPS,),
        in_specs=[pl.BlockSpec((d, gw), lambda g: (0, g)),
                  pl.BlockSpec((gw, 2 * gw), lambda g: (0, 0))],
        out_specs=[pl.BlockSpec((d, gw), lambda g: (0, g)),
                   pl.BlockSpec((d, gw), lambda g: (0, g))],
        out_shape=[out, out],
        compiler_params=_cparams(("parallel",), 32),
        name="fold_channel_dft",
    )(wf, cs)


def _rms_mod_prologue(x_ref, pe_refs, gs_ref, sh_ref, a_sc, tm):
    gs = gs_ref[...]
    sh = sh_ref[...]
    half = x_ref.shape[-1] // 2

    def body(r, carry):
        r0 = pl.multiple_of(r * ROWS, ROWS)
        h = x_ref[pl.ds(r0, ROWS), :]
        if pe_refs is not None:
            per_ref, pec_ref = pe_refs
            row = jnp.broadcast_to(per_ref[pl.ds(r, 1), :], (ROWS, half))
            h = h + jnp.concatenate([row, pec_ref[...]], axis=-1)
        ms = jnp.mean(h * h, axis=-1, keepdims=True)
        a_sc[pl.ds(r0, ROWS), :] = (h * lax.rsqrt(ms + EPS) * gs + sh).astype(BF16)
        return carry

    lax.fori_loop(0, tm // ROWS, body, 0)


def _inproj_kernel(*refs, tm, with_pe, n_main):
    if with_pe:
        x_ref, per_ref, pec_ref, gs_ref, sh_ref, w_ref, wz_ref, p_ref, zf_ref, z_ref, a_sc = refs
        pe_refs = (per_ref, pec_ref)
    else:
        x_ref, gs_ref, sh_ref, w_ref, wz_ref, p_ref, zf_ref, z_ref, a_sc = refs
        pe_refs = None
    j = pl.program_id(2)

    @pl.when(j == 0)
    def _():
        _rms_mod_prologue(x_ref, pe_refs, gs_ref, sh_ref, a_sc, tm)
        z_ref[...] = _dot(a_sc[...], wz_ref[...])

    y = _dot(a_sc[...], w_ref[...])

    @pl.when(j < n_main)
    def _():
        p_ref[...] = y.astype(BF16)

    @pl.when(j >= n_main)
    def _():
        zf_ref[...] = y


def _inproj_call(x, pe, gs, sh, w, wz, tm, tn, n_main_cols):
    b, l, d = x.shape
    n = w.shape[1]
    n_main = n_main_cols // tn
    with_pe = pe is not None
    in_specs = [pl.BlockSpec((None, tm, d), lambda bi, i, j: (bi, i, 0))]
    args = [x]
    if with_pe:
        per, pec = pe
        in_specs += [pl.BlockSpec((tm // GRID_W, d // 2), lambda bi, i, j: (i, 0)),
                     pl.BlockSpec((GRID_W, d // 2), lambda bi, i, j: (0, 0))]
        args += [per, pec]
    in_specs += [pl.BlockSpec((None, 1, d), lambda bi, i, j: (bi, 0, 0)),
                 pl.BlockSpec((None, 1, d), lambda bi, i, j: (bi, 0, 0)),
                 pl.BlockSpec((d, tn), lambda bi, i, j: (0, j)),
                 pl.BlockSpec((d, Z_PAD), lambda bi, i, j: (0, 0))]
    args += [gs, sh, w, wz]
    return pl.pallas_call(
        functools.partial(_inproj_kernel, tm=tm, with_pe=with_pe, n_main=n_main),
        grid=(b, l // tm, n // tn),
        in_specs=in_specs,
        out_specs=[pl.BlockSpec((None, tm, tn), lambda bi, i, j: (bi, i, jnp.minimum(j, n_main - 1))),
                   pl.BlockSpec((None, tm, tn), lambda bi, i, j: (bi, i, jnp.maximum(j - n_main, 0))),
                   pl.BlockSpec((None, tm, Z_PAD), lambda bi, i, j: (bi, i, 0))],
        out_shape=[jax.ShapeDtypeStruct((b, l, n_main_cols), BF16),
                   jax.ShapeDtypeStruct((b, l, n - n_main_cols), F32),
                   jax.ShapeDtypeStruct((b, l, Z_PAD), F32)],
        scratch_shapes=[pltpu.VMEM((tm, d), BF16)],
        compiler_params=_cparams(("parallel", "parallel", "arbitrary"), 48),
        name="input_projection_pe" if with_pe else "input_projection_ctx",
    )(*args)


def _tri_mask(rows, n, fwd):
    row = lax.broadcasted_iota(jnp.int32, (rows, rows), 0)
    col = lax.broadcasted_iota(jnp.int32, (rows, rows), 1)
    order = (col <= row) if fwd else (col >= row)
    if rows == n:
        return order
    return order & ((row // n) == (col // n))


def _gate_cum(z, gw_ref, gb_ref, mask):
    logit = _dot(z.astype(BF16), gw_ref[...]) + gb_ref[...]
    g = (jnp.minimum(logit, 0.0) - jnp.log1p(jnp.exp(-jnp.abs(logit)))) * (1.0 / GATE_NORM)
    tri = jnp.where(mask, 1.0, 0.0).astype(BF16)
    g_hi, g_lo = _split_bf16(g)
    return _dot(tri, g_hi) + _dot(tri, g_lo)


def _gla_kernel(*refs, fwd, n, nsub, nb):
    if fwd:
        (q_ref, k_ref, v_ref, z_ref, gw_ref, gb_ref, s0_ref, ob_ref, r_ref, ng_ref, o_ref, st_sc) = refs
    else:
        (q_ref, k_ref, v_ref, z_ref, gw_ref, gb_ref, s0_ref, o_ref, st_sc) = refs

    @pl.when(pl.program_id(0) == 0)
    def _():
        st_sc[...] = s0_ref[...]

    mask = _tri_mask(n, n, fwd)
    cmask = _tri_mask(nsub * n, n, fwd)
    last = n - 1 if fwd else 0
    scale = GLA_DK ** -0.5
    for bi in range(nb):
        b_all = _gate_cum(z_ref[bi], gw_ref, gb_ref, cmask)
        for cc in (range(nsub) if fwd else reversed(range(nsub))):
            rows = slice(cc * n, (cc + 1) * n)
            for h in range(GLA_HEADS):
                ks = slice(h * GLA_DK, (h + 1) * GLA_DK)
                vs = slice(h * GLA_DV, (h + 1) * GLA_DV)
                bh = b_all[rows, ks]
                bl = bh[last:last + 1, :]
                q = q_ref[bi, rows, ks].astype(F32) * scale
                k = k_ref[bi, rows, ks].astype(F32)
                v = v_ref[bi, rows, vs]
                qd = (q * jnp.exp(bh)).astype(BF16)
                ki = (k * jnp.exp(-bh)).astype(BF16)
                kt = (k * jnp.exp(bl - bh)).astype(BF16)
                att = jnp.where(mask, _dot_nt(qd, ki), 0.0).astype(BF16)
                st = st_sc[bi, h]
                o = _dot(att, v) + _dot_nt(qd, st.astype(BF16))
                st_sc[bi, h] = jnp.exp(bl) * st + _dot_tn(v, kt)
                if fwd:
                    o = o + ob_ref[bi, rows, vs]
                    on = o * lax.rsqrt(jnp.mean(o * o, axis=-1, keepdims=True) + EPS) * ng_ref[...]
                    o_ref[bi, rows, vs] = (on * _silu(r_ref[bi, rows, vs].astype(F32))).astype(BF16)
                else:
                    o_ref[bi, rows, vs] = o


def _gla_call(p, z, gw, gb, s0, fwd, ob=None, ng=None):
    b, l, _ = p.shape
    n = CHUNK
    nsub = GLA_SUBCHUNKS
    rows = n * nsub
    ns = l // rows
    hk = GLA_HEADS * GLA_DK
    hv = GLA_HEADS * GLA_DV
    cidx = (lambda c: c) if fwd else (lambda c: ns - 1 - c)
    in_specs = [pl.BlockSpec((b, rows, hk), lambda c: (0, cidx(c), 0)),
                pl.BlockSpec((b, rows, hk), lambda c: (0, cidx(c), 1)),
                pl.BlockSpec((b, rows, hv), lambda c: (0, cidx(c), 1)),
                pl.BlockSpec((b, rows, Z_PAD), lambda c: (0, cidx(c), 0)),
                pl.BlockSpec((Z_PAD, hk), lambda c: (0, 0)),
                pl.BlockSpec((1, hk), lambda c: (0, 0)),
                pl.BlockSpec((b, GLA_HEADS, GLA_DV, GLA_DK), lambda c: (0, 0, 0, 0))]
    args = [p, p, p, z, gw, gb, s0]
    if fwd:
        in_specs += [pl.BlockSpec((b, rows, hv), lambda c: (0, cidx(c), 0)),
                     pl.BlockSpec((b, rows, hv), lambda c: (0, cidx(c), 2)),
                     pl.BlockSpec((1, GLA_DV), lambda c: (0, 0))]
        args += [ob, p, ng]
    return pl.pallas_call(
        functools.partial(_gla_kernel, fwd=fwd, n=n, nsub=nsub, nb=b),
        grid=(ns,),
        in_specs=in_specs,
        out_specs=pl.BlockSpec((b, rows, hv), lambda c: (0, cidx(c), 0)),
        out_shape=jax.ShapeDtypeStruct((b, l, hv), BF16 if fwd else F32),
        scratch_shapes=[pltpu.VMEM((b, GLA_HEADS, GLA_DV, GLA_DK), F32)],
        compiler_params=_cparams(("arbitrary",), 32),
        name="gla_scan_fwd" if fwd else "gla_scan_bwd",
    )(*args)


def _ctx_state_kernel(k_ref, v_ref, z_ref, gw_ref, gb_ref, s_ref, *, fwd, n):
    mask = _tri_mask(n, n, fwd)
    b = _gate_cum(z_ref[...], gw_ref, gb_ref, mask)
    last = n - 1 if fwd else 0
    for h in range(GLA_HEADS):
        ks = slice(h * GLA_DK, (h + 1) * GLA_DK)
        vs = slice(h * GLA_DV, (h + 1) * GLA_DV)
        bh = b[:, ks]
        kt = (k_ref[:, ks].astype(F32) * jnp.exp(bh[last:last + 1, :] - bh)).astype(BF16)
        s_ref[h] = _dot_tn(v_ref[:, vs], kt)


def _ctx_state_call(pc, zc, gw, gb, fwd):
    b, n, _ = pc.shape
    hk = GLA_HEADS * GLA_DK
    hv = GLA_HEADS * GLA_DV
    return pl.pallas_call(
        functools.partial(_ctx_state_kernel, fwd=fwd, n=n),
        grid=(b,),
        in_specs=[pl.BlockSpec((None, n, hk), lambda bi: (bi, 0, 1)),
                  pl.BlockSpec((None, n, hv), lambda bi: (bi, 0, 1)),
                  pl.BlockSpec((None, n, Z_PAD), lambda bi: (bi, 0, 0)),
                  pl.BlockSpec((Z_PAD, hk), lambda bi: (0, 0)),
                  pl.BlockSpec((1, hk), lambda bi: (0, 0))],
        out_specs=pl.BlockSpec((None, GLA_HEADS, GLA_DV, GLA_DK), lambda bi: (bi, 0, 0, 0)),
        out_shape=jax.ShapeDtypeStruct((b, GLA_HEADS, GLA_DV, GLA_DK), F32),
        compiler_params=_cparams(("parallel",), 32),
        name="ctx_state_fwd" if fwd else "ctx_state_bwd",
    )(pc, pc, zc, gw, gb)


def _fft1_kernel(z_ref, w_ref, o_ref, *, n1, cw):
    for j in range(FFT_COLS):
        zj = z_ref[:, j, :]
        zz = jnp.concatenate([zj[:, :cw], zj[:, cw:]], axis=0).astype(BF16)
        a = _dot(w_ref[...], zz)
        o_ref[0, j] = a[:n1]
        o_ref[1, j] = a[n1:]


def _fft2_kernel(a_ref, m_ref, o_ref):
    for j in range(FFT_COLS):
        aa = jnp.concatenate([a_ref[0, :, j, :], a_ref[1, :, j, :]], axis=0).astype(BF16)
        o_ref[:, j, :] = _dot(m_ref[j], aa)


def _fourier_call(zf, w1, m2, n1, n2):
    b, l, cw2 = zf.shape
    cw = cw2 // 2
    fc = FFT_COLS
    a = pl.pallas_call(
        functools.partial(_fft1_kernel, n1=n1, cw=cw),
        grid=(b, n2 // fc),
        in_specs=[pl.BlockSpec((None, n1, fc, cw2), lambda bi, s: (bi, 0, s, 0)),
                  pl.BlockSpec((2 * n1, 2 * n1), lambda bi, s: (0, 0))],
        out_specs=pl.BlockSpec((None, 2, fc, n1, cw), lambda bi, s: (bi, 0, s, 0, 0)),
        out_shape=jax.ShapeDtypeStruct((b, 2, n2, n1, cw), F32),
        compiler_params=_cparams(("parallel", "parallel"), 48),
        name="position_dft_stage1",
    )(zf.reshape(b, n1, n2, cw2), w1)
    y = pl.pallas_call(
        _fft2_kernel,
        grid=(b, n1 // fc),
        in_specs=[pl.BlockSpec((None, 2, n2, fc, cw), lambda bi, s: (bi, 0, 0, s, 0)),
                  pl.BlockSpec((fc, n2, 2 * n2), lambda bi, s: (s, 0, 0))],
        out_specs=pl.BlockSpec((None, n2, fc, cw), lambda bi, s: (bi, 0, s, 0)),
        out_shape=jax.ShapeDtypeStruct((b, n2, n1, cw), F32),
        compiler_params=_cparams(("parallel", "parallel"), 48),
        name="position_dft_stage2",
    )(a, m2)
    return y.reshape(b, l, cw)


def _outproj_kernel(x1_ref, x2_ref, w1_ref, w2_ref, xr_ref, per_ref, pec_ref, gate_ref, o_ref, *, tm):
    y = _dot(x1_ref[...], w1_ref[...]) + _dot(x2_ref[...].astype(BF16), w2_ref[...])
    first = pl.program_id(2) == 0
    gate = gate_ref[...]
    half = o_ref.shape[-1]
    for r in range(tm // ROWS):
        rows = slice(r * ROWS, (r + 1) * ROWS)
        pe = jnp.where(first, jnp.broadcast_to(per_ref[r:r + 1, :], (ROWS, half)), pec_ref[...])
        o_ref[rows, :] = xr_ref[rows, :] + pe + gate * y[rows, :]


def _outproj_call(x1, x2, w, x, per, pec, gate, tm):
    b, l, d = x.shape
    kh = x1.shape[-1]
    half = d // 2
    return pl.pallas_call(
        functools.partial(_outproj_kernel, tm=tm),
        grid=(b, l // tm, 2),
        in_specs=[pl.BlockSpec((None, tm, kh), lambda bi, i, j: (bi, i, 0)),
                  pl.BlockSpec((None, tm, kh), lambda bi, i, j: (bi, i, 0)),
                  pl.BlockSpec((kh, half), lambda bi, i, j: (0, j)),
                  pl.BlockSpec((kh, half), lambda bi, i, j: (1, j)),
                  pl.BlockSpec((None, tm, half), lambda bi, i, j: (bi, i, j)),
                  pl.BlockSpec((tm // GRID_W, half), lambda bi, i, j: (i, 0)),
                  pl.BlockSpec((GRID_W, half), lambda bi, i, j: (0, 0)),
                  pl.BlockSpec((None, 1, half), lambda bi, i, j: (bi, 0, j))],
        out_specs=pl.BlockSpec((None, tm, half), lambda bi, i, j: (bi, i, j)),
        out_shape=jax.ShapeDtypeStruct((b, l, d), F32),
        compiler_params=_cparams(("parallel", "parallel", "arbitrary"), 48),
        name="output_projection",
    )(x1, x2, w, w, x, per, pec, gate)


def _pair_kernel(*refs, tm, glu):
    if glu:
        x_ref, gs_ref, sh_ref, wa_ref, wb_ref, ba_ref, bb_ref, o_ref, a_sc = refs
    else:
        x_ref, gs_ref, sh_ref, wa_ref, wb_ref, o_ref, a_sc = refs

    @pl.when(pl.program_id(2) == 0)
    def _():
        _rms_mod_prologue(x_ref, None, gs_ref, sh_ref, a_sc, tm)

    a = a_sc[...]
    ya = _dot(a, wa_ref[...])
    yb = _dot(a, wb_ref[...])
    if glu:
        o_ref[...] = ((ya + ba_ref[...]) * jax.nn.sigmoid(yb + bb_ref[...])).astype(BF16)
    else:
        o_ref[...] = (_silu(ya) * yb).astype(BF16)


def _pair_call(x, gs, sh, w, bias, tm, tn, name):
    b, l, d = x.shape
    nh = w.shape[1] // 2
    nj = nh // tn
    glu = bias is not None
    in_specs = [pl.BlockSpec((None, tm, d), lambda bi, i, j: (bi, i, 0)),
                pl.BlockSpec((None, 1, d), lambda bi, i, j: (bi, 0, 0)),
                pl.BlockSpec((None, 1, d), lambda bi, i, j: (bi, 0, 0)),
                pl.BlockSpec((d, tn), lambda bi, i, j: (0, j)),
                pl.BlockSpec((d, tn), lambda bi, i, j: (0, j + nj))]
    args = [x, gs, sh, w, w]
    if glu:
        in_specs += [pl.BlockSpec((1, tn), lambda bi, i, j: (0, j)),
                     pl.BlockSpec((1, tn), lambda bi, i, j: (0, j + nj))]
        args += [bias, bias]
    return pl.pallas_call(
        functools.partial(_pair_kernel, tm=tm, glu=glu),
        grid=(b, l // tm, nj),
        in_specs=in_specs,
        out_specs=pl.BlockSpec((None, tm, tn), lambda bi, i, j: (bi, i, j)),
        out_shape=jax.ShapeDtypeStruct((b, l, nh), BF16),
        scratch_shapes=[pltpu.VMEM((tm, d), BF16)],
        compiler_params=_cparams(("parallel", "parallel", "arbitrary"), 48),
        name=name,
    )(*args)


def _ffn_out_kernel(*refs, final):
    if final:
        x_ref, w_ref, res_ref, gate_ref, g_ref, o_ref = refs
    else:
        x_ref, w_ref, res_ref, gate_ref, o_ref = refs
    y = res_ref[...] + gate_ref[...] * _dot(x_ref[...], w_ref[...])
    if final:
        y = y * lax.rsqrt(jnp.mean(y * y, axis=-1, keepdims=True) + EPS) * g_ref[...]
    o_ref[...] = y


def _ffn_out_call(xh, w, res, gate, final_g, tm):
    b, l, kk = xh.shape
    d = w.shape[1]
    final = final_g is not None
    in_specs = [pl.BlockSpec((None, tm, kk), lambda bi, i: (bi, i, 0)),
                pl.BlockSpec((kk, d), lambda bi, i: (0, 0), pipeline_mode=pl.Buffered(1)),
                pl.BlockSpec((None, tm, d), lambda bi, i: (bi, i, 0)),
                pl.BlockSpec((None, 1, d), lambda bi, i: (bi, 0, 0))]
    args = [xh, w, res, gate]
    if final:
        in_specs.append(pl.BlockSpec((1, d), lambda bi, i: (0, 0)))
        args.append(final_g)
    return pl.pallas_call(
        functools.partial(_ffn_out_kernel, final=final),
        grid=(b, l // tm),
        in_specs=in_specs,
        out_specs=pl.BlockSpec((None, tm, d), lambda bi, i: (bi, i, 0)),
        out_shape=jax.ShapeDtypeStruct((b, l, d), F32),
        compiler_params=_cparams(("parallel", "parallel"), 56),
        name="ffn_output_final_norm" if final else "ffn_output_projection",
    )(*args)


def _conv_ln_prologue(um_ref, up_ref, un_ref, wdw_ref, bdw_ref, lng_ref, lnb_ref, ext_sc, cv_sc, a_sc, tm):
    i = pl.program_id(1)
    d = um_ref.shape[-1]
    halo = CONV_HALO
    ext_sc[0:halo, :] = jnp.where(i > 0, up_ref[...].astype(F32), 0.0)
    ext_sc[halo:halo + tm, :] = um_ref[...].astype(F32)
    ext_sc[halo + tm:, :] = jnp.where(i < pl.num_programs(1) - 1, un_ref[...].astype(F32), 0.0)
    n_a = (CONV_W + 8) // 8
    span = ROWS + 8
    lanes = 128

    def body(r, carry):
        r0 = pl.multiple_of(r * ROWS, ROWS)
        for cb in range(d // lanes):
            cs = slice(cb * lanes, (cb + 1) * lanes)
            e = ext_sc[pl.ds(r0, span + 8 * (n_a - 1)), cs]
            acc = None
            for s in range(8):
                g = None
                for a in range(n_a):
                    m = 8 * a + s
                    if m == 0 or m > CONV_W:
                        continue
                    term = wdw_ref[m:m + 1, cs] * e[8 * a:8 * a + span, :]
                    g = term if g is None else g + term
                gs = g if s == 0 else pltpu.roll(g, span - s, 0)
                acc = gs[:ROWS, :] if acc is None else acc + gs[:ROWS, :]
            cv_sc[pl.ds(r0, ROWS), cs] = acc + bdw_ref[:, cs]
        c = cv_sc[pl.ds(r0, ROWS), :]
        xc = c - jnp.mean(c, axis=-1, keepdims=True)
        y = xc * lax.rsqrt(jnp.mean(xc * xc, axis=-1, keepdims=True) + EPS) * lng_ref[...] + lnb_ref[...]
        a_sc[pl.ds(r0, ROWS), :] = _silu(y).astype(BF16)
        return carry

    lax.fori_loop(0, tm // ROWS, body, 0)


def _conv_pw2_kernel(um_ref, up_ref, un_ref, wdw_ref, bdw_ref, lng_ref, lnb_ref, w_ref, b2_ref, res_ref, gate_ref,
                     o_ref, ext_sc, cv_sc, a_sc, *, tm):
    @pl.when(pl.program_id(2) == 0)
    def _():
        _conv_ln_prologue(um_ref, up_ref, un_ref, wdw_ref, bdw_ref, lng_ref, lnb_ref, ext_sc, cv_sc, a_sc, tm)

    o_ref[...] = res_ref[...] + gate_ref[...] * (_dot(a_sc[...], w_ref[...]) + b2_ref[...])


def _conv_pw2_call(u, wdw, bdw, lng, lnb, w, b2, res, gate, tm, tn):
    b, l, d = u.shape
    halo = CONV_HALO
    hb = tm // halo
    nhb = l // halo
    return pl.pallas_call(
        functools.partial(_conv_pw2_kernel, tm=tm),
        grid=(b, l // tm, d // tn),
        in_specs=[pl.BlockSpec((None, tm, d), lambda bi, i, j: (bi, i, 0)),
                  pl.BlockSpec((None, halo, d), lambda bi, i, j: (bi, jnp.maximum(i * hb - 1, 0), 0)),
                  pl.BlockSpec((None, halo, d), lambda bi, i, j: (bi, jnp.minimum((i + 1) * hb, nhb - 1), 0)),
                  pl.BlockSpec((CONV_W + 1, d), lambda bi, i, j: (0, 0)),
                  pl.BlockSpec((1, d), lambda bi, i, j: (0, 0)),
                  pl.BlockSpec((1, d), lambda bi, i, j: (0, 0)),
                  pl.BlockSpec((1, d), lambda bi, i, j: (0, 0)),
                  pl.BlockSpec((d, tn), lambda bi, i, j: (0, j)),
                  pl.BlockSpec((1, tn), lambda bi, i, j: (0, j)),
                  pl.BlockSpec((None, tm, tn), lambda bi, i, j: (bi, i, j)),
                  pl.BlockSpec((None, 1, tn), lambda bi, i, j: (bi, 0, j))],
        out_specs=pl.BlockSpec((None, tm, tn), lambda bi, i, j: (bi, i, j)),
        out_shape=jax.ShapeDtypeStruct((b, l, d), F32),
        scratch_shapes=[pltpu.VMEM((tm + 2 * halo, d), F32),
                        pltpu.VMEM((tm, d), F32),
                        pltpu.VMEM((tm, d), BF16)],
        compiler_params=_cparams(("parallel", "parallel", "arbitrary"), 48),
        name="conv_ln_pointwise2",
    )(u, u, u, wdw, bdw, lng, lnb, w, b2, res, gate)


def _pos_tables(rows, d):
    nq = d // 4
    omega = 1.0 / (10000.0 ** (jnp.arange(nq, dtype=F32) / nq))
    ar = jnp.arange(rows, dtype=F32)[:, None] * omega[None, :]
    ac = jnp.arange(GRID_W, dtype=F32)[:, None] * omega[None, :]
    per = jnp.concatenate([jnp.sin(ar), jnp.cos(ar)], axis=-1)
    pec = jnp.concatenate([jnp.sin(ac), jnp.cos(ac)], axis=-1)
    return per, pec


def _cos_sin(num, den):
    ang = (num % den).astype(F32) * (2.0 * math.pi / den)
    return jnp.cos(ang), jnp.sin(ang)


def _dft_tables(n1, n2):
    i32 = jnp.int32
    gw = FN_GROUP_W
    cc, sc = _cos_sin(jnp.arange(gw, dtype=i32)[:, None] * jnp.arange(gw, dtype=i32)[None, :], gw)
    chan = jnp.concatenate([cc, -sc], axis=-1)
    c1, s1 = _cos_sin(jnp.arange(n1, dtype=i32)[:, None] * jnp.arange(n1, dtype=i32)[None, :], n1)
    w1 = jnp.concatenate([jnp.concatenate([c1, s1], axis=-1),
                          jnp.concatenate([-s1, c1], axis=-1)], axis=0).astype(BF16)
    k1 = jnp.arange(n1, dtype=i32)[:, None, None]
    k2 = jnp.arange(n2, dtype=i32)[None, :, None]
    nn = jnp.arange(n2, dtype=i32)[None, None, :]
    c2, s2 = _cos_sin(nn * (k1 + n1 * k2), n1 * n2)
    norm = 1.0 / math.sqrt(n1 * n2 * gw)
    m2 = (jnp.concatenate([c2, s2], axis=-1) * norm).astype(BF16)
    return chan, w1, m2


def _split_len(l):
    n1 = 1 << ((l.bit_length() - 1) // 2)
    n2 = l // n1
    assert n1 * n2 == l and n1 % 8 == 0 and n2 % 16 == 0, l
    return n1, n2


def kernel(x, c, ctx, c_ctx, ada_w, ada_b, norm_mix_g, norm_ffn_g, ffn_w_in, ffn_w_out, gm_w_in, gla_gate_w_fwd, gla_gate_b_fwd, gla_gate_w_bwd, gla_gate_b_bwd, gla_norm_g, gm_w_out, cv_w_pw1, cv_b_pw1, cv_w_dw, cv_b_dw, cv_ln_g, cv_ln_b, cv_w_pw2, cv_b_pw2, final_norm_g):
    b, l, d = x.shape
    depth = ada_w.shape[0]
    assert depth == 2 and b + 1 <= 8
    hk = GLA_HEADS * GLA_DK
    hv = GLA_HEADS * GLA_DV
    cw = FN_GROUPS * FN_GROUP_W
    n1, n2 = _split_len(l)
    tm_big = min(1024, l)
    tm_mid = min(512, l)
    tm_small = min(256, l)

    per, pec = _pos_tables(l // GRID_W, d)
    chan, w1, m2 = _dft_tables(n1, n2)

    cin = jnp.zeros((8, d), F32).at[:b].set(c).at[b].set(c_ctx)
    mod = _ada_call(cin, ada_w, ada_b).reshape(depth, 8, 6, d)

    def mod_vecs(layer, rows, g, which):
        shift, scale = mod[layer, rows, 3 * which], mod[layer, rows, 3 * which + 1]
        return (g[None, :] * (1.0 + scale))[:, None, :], shift[:, None, :]

    def gate_vec(layer, rows, which):
        return mod[layer, rows, 3 * which + 2][:, None, :]

    lat = slice(0, b)
    w_in = gm_w_in[0]
    q_off, k_off, v_off, r_off = 0, hk, 2 * hk, 2 * hk + hv
    zf_off = r_off + hv
    f_off = zf_off + 2 * GATE_RANK
    wzr, wzi = _fold_dft_call(w_in[:, f_off:], chan)
    w_main = jnp.concatenate([w_in[:, :zf_off].astype(BF16), wzr, wzi], axis=-1)
    w_z = jnp.pad(w_in[:, zf_off:f_off], ((0, 0), (0, Z_PAD - 2 * GATE_RANK))).astype(BF16)

    def gate_w(gw, off):
        return jnp.pad(gw, ((off, Z_PAD - GATE_RANK - off), (0, 0))).astype(BF16)

    gw_f, gw_b = gate_w(gla_gate_w_fwd[0], 0), gate_w(gla_gate_w_bwd[0], GATE_RANK)
    gb_f, gb_b = gla_gate_b_fwd[0][None, :], gla_gate_b_bwd[0][None, :]

    gs, sh = mod_vecs(0, lat, norm_mix_g[0], 0)
    gs_c, sh_c = mod_vecs(0, slice(b, b + 1), norm_mix_g[0], 0)
    gs_c, sh_c = jnp.broadcast_to(gs_c, (b, 1, d)), jnp.broadcast_to(sh_c, (b, 1, d))

    tn = 512
    pc, _, zc = _inproj_call(ctx, None, gs_c, sh_c, w_main, w_z, ctx.shape[1], tn, zf_off)
    p, zf, z = _inproj_call(x, (per, pec), gs, sh, w_main, w_z, tm_big, tn, zf_off)

    s0_f = _ctx_state_call(pc, zc, gw_f, gb_f, True)
    s0_b = _ctx_state_call(pc, zc, gw_b, gb_b, False)
    ob = _gla_call(p, z, gw_b, gb_b, s0_b, False)
    og = _gla_call(p, z, gw_f, gb_f, s0_f, True, ob=ob, ng=gla_norm_g[0][None, :])

    yf = _fourier_call(zf, w1, m2, n1, n2)

    h = _outproj_call(og, yf, gm_w_out[0].astype(BF16), x, per, pec, gate_vec(0, lat, 0), tm_big)

    def ffn(hin, layer, final_g=None):
        gs2, sh2 = mod_vecs(layer, lat, norm_ffn_g[layer], 1)
        hid = _pair_call(hin, gs2, sh2, ffn_w_in[layer].astype(BF16), None, tm_big, 512, "ffn_swiglu_in")
        return _ffn_out_call(hid, ffn_w_out[layer].astype(BF16), hin, gate_vec(layer, lat, 1), final_g, tm_small)

    h = ffn(h, 0)

    gs, sh = mod_vecs(1, lat, norm_mix_g[1], 0)
    u = _pair_call(h, gs, sh, cv_w_pw1[0].astype(BF16), cv_b_pw1[0][None, :], tm_big, 512, "pointwise1_glu")
    wdw = jnp.pad(cv_w_dw[0], ((1, 0), (0, 0)))
    h = _conv_pw2_call(u, wdw, cv_b_dw[0][None, :], cv_ln_g[0][None, :], cv_ln_b[0][None, :],
                       cv_w_pw2[0].astype(BF16), cv_b_pw2[0][None, :], h, gate_vec(1, lat, 0), tm_mid, 1024)
    return ffn(h, 1, final_g=final_norm_g[None, :])
```

```python
import functools
import math

import jax
import jax.numpy as jnp
from jax import lax
from jax.experimental import pallas as pl
from jax.experimental.pallas import tpu as pltpu

F32 = jnp.float32
BF16 = jnp.bfloat16

EPS = 1e-6
GRID_W = 64
GLA_HEADS = 4
GLA_DK = 128
GLA_DV = 256
GATE_RANK = 16
GATE_NORM = 16.0
CHUNK = 64
GLA_SUBCHUNKS = 2
FN_GROUPS = 4
FN_GROUP_W = 256
FFT_COLS = 8
CONV_W = 31
CONV_HALO = 16
ROWS = 64
Z_PAD = 128

_MIB = 1024 * 1024


def _cparams(sem, vmem_mib):
    return pltpu.CompilerParams(dimension_semantics=sem, vmem_limit_bytes=vmem_mib * _MIB)


def _dot(a, b):
    return jnp.dot(a, b, preferred_element_type=F32)


def _dot_nt(a, b):
    return lax.dot_general(a, b, (((1,), (1,)), ((), ())), preferred_element_type=F32)


def _dot_tn(a, b):
    return lax.dot_general(a, b, (((0,), (0,)), ((), ())), preferred_element_type=F32)


def _silu(x):
    return x * jax.nn.sigmoid(x)


def _ada_kernel(c_ref, w_ref, b_ref, o_ref):
    s = _silu(c_ref[...]).astype(BF16)
    o_ref[...] = _dot(s, w_ref[...].astype(BF16)) + b_ref[...]


def _ada_call(cin, ada_w, ada_b):
    depth, d, n = ada_w.shape
    tn = 1024
    return pl.pallas_call(
        _ada_kernel,
        grid=(depth, n // tn),
        in_specs=[pl.BlockSpec((8, d), lambda l, j: (0, 0)),
                  pl.BlockSpec((None, d, tn), lambda l, j: (l, 0, j)),
                  pl.BlockSpec((None, 1, tn), lambda l, j: (l, 0, j))],
        out_specs=pl.BlockSpec((None, 8, tn), lambda l, j: (l, 0, j)),
        out_shape=jax.ShapeDtypeStruct((depth, 8, n), F32),
        compiler_params=_cparams(("parallel", "parallel"), 40),
        name="ada_modulation",
    )(cin, ada_w, ada_b.reshape(depth, 1, n))


def _split_bf16(v):
    hi = v.astype(BF16)
    lo = (v - hi.astype(F32)).astype(BF16)
    return hi, lo


def _fold_dft_kernel(wf_ref, cs_ref, zr_ref, zi_ref):
    w_hi, w_lo = _split_bf16(wf_ref[...])
    c_hi, c_lo = _split_bf16(cs_ref[...])
    r = _dot(w_hi, c_hi) + _dot(w_hi, c_lo) + _dot(w_lo, c_hi)
    zr_ref[...] = r[:, :FN_GROUP_W].astype(BF16)
    zi_ref[...] = r[:, FN_GROUP_W:].astype(BF16)


def _fold_dft_call(wf, cs):
    d = wf.shape[0]
    gw = FN_GROUP_W
    out = jax.ShapeDtypeStruct((d, FN_GROUPS * gw), BF16)
    return pl.pallas_call(
        _fold_dft_kernel,
        grid=(FN_GROUPS,),
        in_specs=[pl.BlockSpec((d, gw), lambda g: (0, g)),
                  pl.BlockSpec((gw, 2 * gw), lambda g: (0, 0))],
        out_specs=[pl.BlockSpec((d, gw), lambda g: (0, g)),
                   pl.BlockSpec((d, gw), lambda g: (0, g))],
        out_shape=[out, out],
        compiler_params=_cparams(("parallel",), 32),
        name="fold_channel_dft",
    )(wf, cs)


def _rms_mod(h, gs, sh):
    return (h * lax.rsqrt(jnp.mean(h * h, axis=-1, keepdims=True) + EPS) * gs + sh).astype(BF16)


def _rms_mod_prologue(x_ref, pe_refs, gs_ref, sh_ref, a_sc, tm):
    gs = gs_ref[...]
    sh = sh_ref[...]
    half = x_ref.shape[-1] // 2

    def body(r, carry):
        r0 = pl.multiple_of(r * ROWS, ROWS)
        h = x_ref[pl.ds(r0, ROWS), :]
        if pe_refs is not None:
            per_ref, pec_ref = pe_refs
            row = jnp.broadcast_to(per_ref[pl.ds(r, 1), :], (ROWS, half))
            h = h + jnp.concatenate([row, pec_ref[...]], axis=-1)
        a_sc[pl.ds(r0, ROWS), :] = _rms_mod(h, gs, sh)
        return carry

    lax.fori_loop(0, tm // ROWS, body, 0)


def _inproj_kernel(*refs, tm, with_pe, n_main):
    if with_pe:
        x_ref, per_ref, pec_ref, gs_ref, sh_ref, w_ref, wz_ref, p_ref, zf_ref, z_ref, a_sc = refs
        pe_refs = (per_ref, pec_ref)
    else:
        x_ref, gs_ref, sh_ref, w_ref, wz_ref, p_ref, zf_ref, z_ref, a_sc = refs
        pe_refs = None
    j = pl.program_id(2)

    @pl.when(j == 0)
    def _():
        _rms_mod_prologue(x_ref, pe_refs, gs_ref, sh_ref, a_sc, tm)
        z_ref[...] = _dot(a_sc[...], wz_ref[...])

    y = _dot(a_sc[...], w_ref[...])

    @pl.when(j < n_main)
    def _():
        p_ref[...] = y.astype(BF16)

    @pl.when(j >= n_main)
    def _():
        zf_ref[...] = y


def _inproj_call(x, pe, gs, sh, w, wz, tm, tn, n_main_cols):
    b, l, d = x.shape
    n = w.shape[1]
    n_main = n_main_cols // tn
    with_pe = pe is not None
    in_specs = [pl.BlockSpec((None, tm, d), lambda bi, i, j: (bi, i, 0))]
    args = [x]
    if with_pe:
        per, pec = pe
        in_specs += [pl.BlockSpec((tm // GRID_W, d // 2), lambda bi, i, j: (i, 0)),
                     pl.BlockSpec((GRID_W, d // 2), lambda bi, i, j: (0, 0))]
        args += [per, pec]
    in_specs += [pl.BlockSpec((None, 1, d), lambda bi, i, j: (bi, 0, 0)),
                 pl.BlockSpec((None, 1, d), lambda bi, i, j: (bi, 0, 0)),
                 pl.BlockSpec((d, tn), lambda bi, i, j: (0, j)),
                 pl.BlockSpec((d, Z_PAD), lambda bi, i, j: (0, 0))]
    args += [gs, sh, w, wz]
    return pl.pallas_call(
        functools.partial(_inproj_kernel, tm=tm, with_pe=with_pe, n_main=n_main),
        grid=(b, l // tm, n // tn),
        in_specs=in_specs,
        out_specs=[pl.BlockSpec((None, tm, tn), lambda bi, i, j: (bi, i, jnp.minimum(j, n_main - 1))),
                   pl.BlockSpec((None, tm, tn), lambda bi, i, j: (bi, i, jnp.maximum(j - n_main, 0))),
                   pl.BlockSpec((None, tm, Z_PAD), lambda bi, i, j: (bi, i, 0))],
        out_shape=[jax.ShapeDtypeStruct((b, l, n_main_cols), BF16),
                   jax.ShapeDtypeStruct((b, l, n - n_main_cols), F32),
                   jax.ShapeDtypeStruct((b, l, Z_PAD), F32)],
        scratch_shapes=[pltpu.VMEM((tm, d), BF16)],
        compiler_params=_cparams(("parallel", "parallel", "arbitrary"), 56),
        name="input_projection_pe" if with_pe else "input_projection_ctx",
    )(*args)


def _tri_mask(rows, n, fwd):
    row = lax.broadcasted_iota(jnp.int32, (rows, rows), 0)
    col = lax.broadcasted_iota(jnp.int32, (rows, rows), 1)
    order = (col <= row) if fwd else (col >= row)
    if rows == n:
        return order
    return order & ((row // n) == (col // n))


def _gate_cum(z, gw_ref, gb_ref, mask):
    logit = _dot(z.astype(BF16), gw_ref[...]) + gb_ref[...]
    g = (jnp.minimum(logit, 0.0) - jnp.log1p(jnp.exp(-jnp.abs(logit)))) * (1.0 / GATE_NORM)
    tri = jnp.where(mask, 1.0, 0.0).astype(BF16)
    g_hi, g_lo = _split_bf16(g)
    return _dot(tri, g_hi) + _dot(tri, g_lo)


def _gla_kernel(*refs, fwd, n, nsub, nb):
    if fwd:
        (q_ref, k_ref, v_ref, z_ref, gw_ref, gb_ref, s0_ref, ob_ref, r_ref, ng_ref, o_ref, st_sc) = refs
    else:
        (q_ref, k_ref, v_ref, z_ref, gw_ref, gb_ref, s0_ref, o_ref, st_sc) = refs

    @pl.when(pl.program_id(0) == 0)
    def _():
        st_sc[...] = s0_ref[...]

    mask = _tri_mask(n, n, fwd)
    cmask = _tri_mask(nsub * n, n, fwd)
    last = n - 1 if fwd else 0
    scale = GLA_DK ** -0.5
    for bi in range(nb):
        b_all = _gate_cum(z_ref[bi], gw_ref, gb_ref, cmask)
        for cc in (range(nsub) if fwd else reversed(range(nsub))):
            rows = slice(cc * n, (cc + 1) * n)
            for h in range(GLA_HEADS):
                ks = slice(h * GLA_DK, (h + 1) * GLA_DK)
                vs = slice(h * GLA_DV, (h + 1) * GLA_DV)
                bh = b_all[rows, ks]
                bl = bh[last:last + 1, :]
                q = q_ref[bi, rows, ks].astype(F32) * scale
                k = k_ref[bi, rows, ks].astype(F32)
                v = v_ref[bi, rows, vs]
                qd = (q * jnp.exp(bh)).astype(BF16)
                ki = (k * jnp.exp(-bh)).astype(BF16)
                kt = (k * jnp.exp(bl - bh)).astype(BF16)
                att = jnp.where(mask, _dot_nt(qd, ki), 0.0).astype(BF16)
                st = st_sc[bi, h]
                o = _dot(att, v) + _dot_nt(qd, st.astype(BF16))
                st_sc[bi, h] = jnp.exp(bl) * st + _dot_tn(v, kt)
                if fwd:
                    o = o + ob_ref[bi, rows, vs]
                    on = o * lax.rsqrt(jnp.mean(o * o, axis=-1, keepdims=True) + EPS) * ng_ref[...]
                    o_ref[bi, rows, vs] = (on * _silu(r_ref[bi, rows, vs].astype(F32))).astype(BF16)
                else:
                    o_ref[bi, rows, vs] = o


def _gla_call(p, z, gw, gb, s0, fwd, ob=None, ng=None):
    b, l, _ = p.shape
    n = CHUNK
    nsub = GLA_SUBCHUNKS
    rows = n * nsub
    ns = l // rows
    hk = GLA_HEADS * GLA_DK
    hv = GLA_HEADS * GLA_DV
    cidx = (lambda c: c) if fwd else (lambda c: ns - 1 - c)
    in_specs = [pl.BlockSpec((b, rows, hk), lambda c: (0, cidx(c), 0)),
                pl.BlockSpec((b, rows, hk), lambda c: (0, cidx(c), 1)),
                pl.BlockSpec((b, rows, hv), lambda c: (0, cidx(c), 1)),
                pl.BlockSpec((b, rows, Z_PAD), lambda c: (0, cidx(c), 0)),
                pl.BlockSpec((Z_PAD, hk), lambda c: (0, 0)),
                pl.BlockSpec((1, hk), lambda c: (0, 0)),
                pl.BlockSpec((b, GLA_HEADS, GLA_DV, GLA_DK), lambda c: (0, 0, 0, 0))]
    args = [p, p, p, z, gw, gb, s0]
    if fwd:
        in_specs += [pl.BlockSpec((b, rows, hv), lambda c: (0, cidx(c), 0)),
                     pl.BlockSpec((b, rows, hv), lambda c: (0, cidx(c), 2)),
                     pl.BlockSpec((1, GLA_DV), lambda c: (0, 0))]
        args += [ob, p, ng]
    return pl.pallas_call(
        functools.partial(_gla_kernel, fwd=fwd, n=n, nsub=nsub, nb=b),
        grid=(ns,),
        in_specs=in_specs,
        out_specs=pl.BlockSpec((b, rows, hv), lambda c: (0, cidx(c), 0)),
        out_shape=jax.ShapeDtypeStruct((b, l, hv), BF16 if fwd else F32),
        scratch_shapes=[pltpu.VMEM((b, GLA_HEADS, GLA_DV, GLA_DK), F32)],
        compiler_params=_cparams(("arbitrary",), 32),
        name="gla_scan_fwd" if fwd else "gla_scan_bwd",
    )(*args)


def _ctx_state_kernel(k_ref, v_ref, z_ref, gw_ref, gb_ref, s_ref, *, fwd, n):
    mask = _tri_mask(n, n, fwd)
    b = _gate_cum(z_ref[...], gw_ref, gb_ref, mask)
    last = n - 1 if fwd else 0
    for h in range(GLA_HEADS):
        ks = slice(h * GLA_DK, (h + 1) * GLA_DK)
        vs = slice(h * GLA_DV, (h + 1) * GLA_DV)
        bh = b[:, ks]
        kt = (k_ref[:, ks].astype(F32) * jnp.exp(bh[last:last + 1, :] - bh)).astype(BF16)
        s_ref[h] = _dot_tn(v_ref[:, vs], kt)


def _ctx_state_call(pc, zc, gw, gb, fwd):
    b, n, _ = pc.shape
    hk = GLA_HEADS * GLA_DK
    hv = GLA_HEADS * GLA_DV
    return pl.pallas_call(
        functools.partial(_ctx_state_kernel, fwd=fwd, n=n),
        grid=(b,),
        in_specs=[pl.BlockSpec((None, n, hk), lambda bi: (bi, 0, 1)),
                  pl.BlockSpec((None, n, hv), lambda bi: (bi, 0, 1)),
                  pl.BlockSpec((None, n, Z_PAD), lambda bi: (bi, 0, 0)),
                  pl.BlockSpec((Z_PAD, hk), lambda bi: (0, 0)),
                  pl.BlockSpec((1, hk), lambda bi: (0, 0))],
        out_specs=pl.BlockSpec((None, GLA_HEADS, GLA_DV, GLA_DK), lambda bi: (bi, 0, 0, 0)),
        out_shape=jax.ShapeDtypeStruct((b, GLA_HEADS, GLA_DV, GLA_DK), F32),
        compiler_params=_cparams(("parallel",), 32),
        name="ctx_state_fwd" if fwd else "ctx_state_bwd",
    )(pc, pc, zc, gw, gb)


def _fft1_kernel(z_ref, w_ref, o_ref, *, n1, cw):
    for j in range(FFT_COLS):
        zj = z_ref[:, j, :]
        zz = jnp.concatenate([zj[:, :cw], zj[:, cw:]], axis=0).astype(BF16)
        a = _dot(w_ref[...], zz)
        o_ref[0, j] = a[:n1]
        o_ref[1, j] = a[n1:]


def _fft2_kernel(a_ref, m_ref, o_ref):
    for j in range(FFT_COLS):
        aa = jnp.concatenate([a_ref[0, :, j, :], a_ref[1, :, j, :]], axis=0).astype(BF16)
        o_ref[:, j, :] = _dot(m_ref[j], aa)


def _fourier_call(zf, w1, m2, n1, n2):
    b, l, cw2 = zf.shape
    cw = cw2 // 2
    fc = FFT_COLS
    a = pl.pallas_call(
        functools.partial(_fft1_kernel, n1=n1, cw=cw),
        grid=(b, n2 // fc),
        in_specs=[pl.BlockSpec((None, n1, fc, cw2), lambda bi, s: (bi, 0, s, 0)),
                  pl.BlockSpec((2 * n1, 2 * n1), lambda bi, s: (0, 0))],
        out_specs=pl.BlockSpec((None, 2, fc, n1, cw), lambda bi, s: (bi, 0, s, 0, 0)),
        out_shape=jax.ShapeDtypeStruct((b, 2, n2, n1, cw), F32),
        compiler_params=_cparams(("parallel", "parallel"), 48),
        name="position_dft_stage1",
    )(zf.reshape(b, n1, n2, cw2), w1)
    y = pl.pallas_call(
        _fft2_kernel,
        grid=(b, n1 // fc),
        in_specs=[pl.BlockSpec((None, 2, n2, fc, cw), lambda bi, s: (bi, 0, 0, s, 0)),
                  pl.BlockSpec((fc, n2, 2 * n2), lambda bi, s: (s, 0, 0))],
        out_specs=pl.BlockSpec((None, n2, fc, cw), lambda bi, s: (bi, 0, s, 0)),
        out_shape=jax.ShapeDtypeStruct((b, n2, n1, cw), F32),
        compiler_params=_cparams(("parallel", "parallel"), 48),
        name="position_dft_stage2",
    )(a, m2)
    return y.reshape(b, l, cw)


def _outproj_kernel(x1_ref, x2_ref, w_ref, xr_ref, per_ref, pec_ref, gate_ref, gsn_ref, shn_ref, o_ref, an_ref, *, tm):
    kh = x1_ref.shape[-1]
    half = o_ref.shape[-1] // 2
    y = _dot(x1_ref[...], w_ref[:kh, :]) + _dot(x2_ref[...].astype(BF16), w_ref[kh:, :])
    gate = gate_ref[...]
    for r in range(tm // ROWS):
        rows = slice(r * ROWS, (r + 1) * ROWS)
        pe = jnp.concatenate([jnp.broadcast_to(per_ref[r:r + 1, :], (ROWS, half)), pec_ref[...]], axis=-1)
        h = xr_ref[rows, :] + pe + gate * y[rows, :]
        o_ref[rows, :] = h
        an_ref[rows, :] = _rms_mod(h, gsn_ref[...], shn_ref[...])


def _outproj_call(x1, x2, w, x, per, pec, gate, gsn, shn, tm):
    b, l, d = x.shape
    kh = x1.shape[-1]
    half = d // 2
    bvec = pl.BlockSpec((None, 1, d), lambda bi, i: (bi, 0, 0))
    return pl.pallas_call(
        functools.partial(_outproj_kernel, tm=tm),
        grid=(b, l // tm),
        in_specs=[pl.BlockSpec((None, tm, kh), lambda bi, i: (bi, i, 0)),
                  pl.BlockSpec((None, tm, kh), lambda bi, i: (bi, i, 0)),
                  pl.BlockSpec((2 * kh, d), lambda bi, i: (0, 0), pipeline_mode=pl.Buffered(1)),
                  pl.BlockSpec((None, tm, d), lambda bi, i: (bi, i, 0)),
                  pl.BlockSpec((tm // GRID_W, half), lambda bi, i: (i, 0)),
                  pl.BlockSpec((GRID_W, half), lambda bi, i: (0, 0)),
                  bvec, bvec, bvec],
        out_specs=[pl.BlockSpec((None, tm, d), lambda bi, i: (bi, i, 0)),
                   pl.BlockSpec((None, tm, d), lambda bi, i: (bi, i, 0))],
        out_shape=[jax.ShapeDtypeStruct((b, l, d), F32), jax.ShapeDtypeStruct((b, l, d), BF16)],
        compiler_params=_cparams(("parallel", "parallel"), 56),
        name="output_projection",
    )(x1, x2, w, x, per, pec, gate, gsn, shn)


def _pair_kernel(*refs, glu):
    if glu:
        a_ref, wa_ref, wb_ref, ba_ref, bb_ref, o_ref = refs
    else:
        a_ref, wa_ref, wb_ref, o_ref = refs
    a = a_ref[...]
    ya = _dot(a, wa_ref[...])
    yb = _dot(a, wb_ref[...])
    if glu:
        o_ref[...] = ((ya + ba_ref[...]) * jax.nn.sigmoid(yb + bb_ref[...])).astype(BF16)
    else:
        o_ref[...] = (_silu(ya) * yb).astype(BF16)


def _pair_call(a, w, bias, tm, tn, name):
    b, l, d = a.shape
    nh = w.shape[1] // 2
    nj = nh // tn
    glu = bias is not None
    in_specs = [pl.BlockSpec((None, tm, d), lambda bi, i, j: (bi, i, 0)),
                pl.BlockSpec((d, tn), lambda bi, i, j: (0, j)),
                pl.BlockSpec((d, tn), lambda bi, i, j: (0, j + nj))]
    args = [a, w, w]
    if glu:
        in_specs += [pl.BlockSpec((1, tn), lambda bi, i, j: (0, j)),
                     pl.BlockSpec((1, tn), lambda bi, i, j: (0, j + nj))]
        args += [bias, bias]
    return pl.pallas_call(
        functools.partial(_pair_kernel, glu=glu),
        grid=(b, l // tm, nj),
        in_specs=in_specs,
        out_specs=pl.BlockSpec((None, tm, tn), lambda bi, i, j: (bi, i, j)),
        out_shape=jax.ShapeDtypeStruct((b, l, nh), BF16),
        compiler_params=_cparams(("parallel", "parallel", "arbitrary"), 48),
        name=name,
    )(*args)


def _ffn_out_kernel(*refs, final):
    if final:
        x_ref, w_ref, res_ref, gate_ref, g_ref, o_ref = refs
    else:
        x_ref, w_ref, res_ref, gate_ref, gsn_ref, shn_ref, o_ref, an_ref = refs
    y = res_ref[...] + gate_ref[...] * _dot(x_ref[...], w_ref[...])
    if final:
        o_ref[...] = y * lax.rsqrt(jnp.mean(y * y, axis=-1, keepdims=True) + EPS) * g_ref[...]
    else:
        o_ref[...] = y
        an_ref[...] = _rms_mod(y, gsn_ref[...], shn_ref[...])


def _ffn_out_call(xh, w, res, gate, final_g, nxt, tm):
    b, l, kk = xh.shape
    d = w.shape[1]
    final = final_g is not None
    row = pl.BlockSpec((None, tm, d), lambda bi, i: (bi, i, 0))
    bvec = pl.BlockSpec((None, 1, d), lambda bi, i: (bi, 0, 0))
    in_specs = [pl.BlockSpec((None, tm, kk), lambda bi, i: (bi, i, 0)),
                pl.BlockSpec((kk, d), lambda bi, i: (0, 0), pipeline_mode=pl.Buffered(1)),
                row, bvec]
    args = [xh, w, res, gate]
    if final:
        in_specs.append(pl.BlockSpec((1, d), lambda bi, i: (0, 0)))
        args.append(final_g)
        out_specs, out_shape = row, jax.ShapeDtypeStruct((b, l, d), F32)
    else:
        in_specs += [bvec, bvec]
        args += list(nxt)
        out_specs = [row, row]
        out_shape = [jax.ShapeDtypeStruct((b, l, d), F32), jax.ShapeDtypeStruct((b, l, d), BF16)]
    return pl.pallas_call(
        functools.partial(_ffn_out_kernel, final=final),
        grid=(b, l // tm),
        in_specs=in_specs,
        out_specs=out_specs,
        out_shape=out_shape,
        compiler_params=_cparams(("parallel", "parallel"), 56),
        name="ffn_output_final_norm" if final else "ffn_output_projection",
    )(*args)


def _conv_ln_chunk(r, ext_sc, wdw_ref, bdw_ref, lng_ref, lnb_ref, cv_sc, a_dst):
    d = cv_sc.shape[-1]
    n_q = (CONV_W + 8) // 8
    span = ROWS + 8
    lanes = 128
    r0 = pl.multiple_of(r * ROWS, ROWS)
    for cb in range(d // lanes):
        cs = slice(cb * lanes, (cb + 1) * lanes)
        e = ext_sc[pl.ds(r0, span + 8 * (n_q - 1)), cs]
        acc = None
        for s in range(8):
            g = None
            for q in range(n_q):
                m = 8 * q + s
                if m == 0 or m > CONV_W:
                    continue
                term = wdw_ref[m:m + 1, cs] * e[8 * q:8 * q + span, :]
                g = term if g is None else g + term
            gs = g if s == 0 else pltpu.roll(g, span - s, 0)
            acc = gs[:ROWS, :] if acc is None else acc + gs[:ROWS, :]
        cv_sc[pl.ds(r0, ROWS), cs] = acc + bdw_ref[:, cs]
    c = cv_sc[pl.ds(r0, ROWS), :]
    xc = c - jnp.mean(c, axis=-1, keepdims=True)
    y = xc * lax.rsqrt(jnp.mean(xc * xc, axis=-1, keepdims=True) + EPS) * lng_ref[...] + lnb_ref[...]
    a_dst[pl.ds(r0, ROWS), :] = _silu(y).astype(BF16)


def _conv_pw2_kernel(um_ref, up_ref, un_ref, wdw_ref, bdw_ref, lng_ref, lnb_ref, w_ref, b2_ref, res_ref, gate_ref,
                     gsn_ref, shn_ref, o_ref, an_ref, ext_sc, cv_sc, a_sc, *, tm, nt):
    i = pl.program_id(1)
    ci = jnp.minimum(i, nt - 1)
    halo = CONV_HALO

    @pl.when(i == 0)
    def _():
        a_sc[1] = jnp.zeros(a_sc.shape[1:], BF16)

    ext_sc[0:halo, :] = jnp.where(ci > 0, up_ref[...].astype(F32), 0.0)
    ext_sc[halo:halo + tm, :] = um_ref[...].astype(F32)
    ext_sc[halo + tm:, :] = jnp.where(ci < nt - 1, un_ref[...].astype(F32), 0.0)
    cur = i % 2
    prev = 1 - cur
    n_slices = w_ref.shape[0]
    cols = w_ref.shape[2]

    def body(n, carry):
        c0 = pl.multiple_of(n * cols, cols)
        y = _dot(a_sc[prev], w_ref[n]) + b2_ref[:, pl.ds(c0, cols)]
        o_ref[:, pl.ds(c0, cols)] = res_ref[:, pl.ds(c0, cols)] + gate_ref[:, pl.ds(c0, cols)] * y
        _conv_ln_chunk(n, ext_sc, wdw_ref, bdw_ref, lng_ref, lnb_ref, cv_sc, a_sc.at[cur])
        return carry

    lax.fori_loop(0, n_slices, body, 0)

    def act(r, carry):
        r0 = pl.multiple_of(r * ROWS, ROWS)
        an_ref[pl.ds(r0, ROWS), :] = _rms_mod(o_ref[pl.ds(r0, ROWS), :], gsn_ref[...], shn_ref[...])
        return carry

    lax.fori_loop(0, tm // ROWS, act, 0)


def _conv_pw2_call(u, wdw, bdw, lng, lnb, w, b2, res, gate, gsn, shn, tm):
    b, l, d = u.shape
    halo = CONV_HALO
    nt = l // tm
    hb = tm // halo
    nhb = l // halo
    n_slices = tm // ROWS
    cols = d // n_slices
    w3 = w.reshape(d, n_slices, cols).transpose(1, 0, 2)

    def tile(i):
        return jnp.minimum(i, nt - 1)

    def prev(i):
        return jnp.maximum(i - 1, 0)

    vec = pl.BlockSpec((1, d), lambda bi, i: (0, 0))
    bvec = pl.BlockSpec((None, 1, d), lambda bi, i: (bi, 0, 0))
    return pl.pallas_call(
        functools.partial(_conv_pw2_kernel, tm=tm, nt=nt),
        grid=(b, nt + 1),
        in_specs=[pl.BlockSpec((None, tm, d), lambda bi, i: (bi, tile(i), 0)),
                  pl.BlockSpec((None, halo, d), lambda bi, i: (bi, jnp.maximum(tile(i) * hb - 1, 0), 0)),
                  pl.BlockSpec((None, halo, d), lambda bi, i: (bi, jnp.minimum((tile(i) + 1) * hb, nhb - 1), 0)),
                  pl.BlockSpec((CONV_W + 1, d), lambda bi, i: (0, 0)),
                  vec, vec, vec,
                  pl.BlockSpec((n_slices, d, cols), lambda bi, i: (0, 0, 0), pipeline_mode=pl.Buffered(1)),
                  vec,
                  pl.BlockSpec((None, tm, d), lambda bi, i: (bi, prev(i), 0)),
                  bvec, bvec, bvec],
        out_specs=[pl.BlockSpec((None, tm, d), lambda bi, i: (bi, prev(i), 0)),
                   pl.BlockSpec((None, tm, d), lambda bi, i: (bi, prev(i), 0))],
        out_shape=[jax.ShapeDtypeStruct((b, l, d), F32), jax.ShapeDtypeStruct((b, l, d), BF16)],
        scratch_shapes=[pltpu.VMEM((tm + 2 * halo, d), F32),
                        pltpu.VMEM((tm, d), F32),
                        pltpu.VMEM((2, tm, d), BF16)],
        compiler_params=_cparams(("parallel", "arbitrary"), 56),
        name="conv_ln_pointwise2",
    )(u, u, u, wdw, bdw, lng, lnb, w3, b2, res, gate, gsn, shn)


def _pos_tables(rows, d):
    nq = d // 4
    omega = 1.0 / (10000.0 ** (jnp.arange(nq, dtype=F32) / nq))
    ar = jnp.arange(rows, dtype=F32)[:, None] * omega[None, :]
    ac = jnp.arange(GRID_W, dtype=F32)[:, None] * omega[None, :]
    per = jnp.concatenate([jnp.sin(ar), jnp.cos(ar)], axis=-1)
    pec = jnp.concatenate([jnp.sin(ac), jnp.cos(ac)], axis=-1)
    return per, pec


def _cos_sin(num, den):
    ang = (num % den).astype(F32) * (2.0 * math.pi / den)
    return jnp.cos(ang), jnp.sin(ang)


def _dft_tables(n1, n2):
    i32 = jnp.int32
    gw = FN_GROUP_W
    cc, sc = _cos_sin(jnp.arange(gw, dtype=i32)[:, None] * jnp.arange(gw, dtype=i32)[None, :], gw)
    chan = jnp.concatenate([cc, -sc], axis=-1)
    c1, s1 = _cos_sin(jnp.arange(n1, dtype=i32)[:, None] * jnp.arange(n1, dtype=i32)[None, :], n1)
    w1 = jnp.concatenate([jnp.concatenate([c1, s1], axis=-1),
                          jnp.concatenate([-s1, c1], axis=-1)], axis=0).astype(BF16)
    k1 = jnp.arange(n1, dtype=i32)[:, None, None]
    k2 = jnp.arange(n2, dtype=i32)[None, :, None]
    nn = jnp.arange(n2, dtype=i32)[None, None, :]
    c2, s2 = _cos_sin(nn * (k1 + n1 * k2), n1 * n2)
    norm = 1.0 / math.sqrt(n1 * n2 * gw)
    m2 = (jnp.concatenate([c2, s2], axis=-1) * norm).astype(BF16)
    return chan, w1, m2


def _split_len(l):
    n1 = 1 << ((l.bit_length() - 1) // 2)
    n2 = l // n1
    assert n1 * n2 == l and n1 % 8 == 0 and n2 % 16 == 0, l
    return n1, n2


def kernel(x, c, ctx, c_ctx, ada_w, ada_b, norm_mix_g, norm_ffn_g, ffn_w_in, ffn_w_out, gm_w_in, gla_gate_w_fwd, gla_gate_b_fwd, gla_gate_w_bwd, gla_gate_b_bwd, gla_norm_g, gm_w_out, cv_w_pw1, cv_b_pw1, cv_w_dw, cv_b_dw, cv_ln_g, cv_ln_b, cv_w_pw2, cv_b_pw2, final_norm_g):
    b, l, d = x.shape
    depth = ada_w.shape[0]
    assert depth == 2 and b + 1 <= 8
    hk = GLA_HEADS * GLA_DK
    hv = GLA_HEADS * GLA_DV
    n1, n2 = _split_len(l)
    tm_huge = min(2048, l)
    tm_big = min(1024, l)
    tm_mid = min(512, l)
    tm_small = min(256, l)

    per, pec = _pos_tables(l // GRID_W, d)
    chan, w1, m2 = _dft_tables(n1, n2)

    cin = jnp.zeros((8, d), F32).at[:b].set(c).at[b].set(c_ctx)
    mod = _ada_call(cin, ada_w, ada_b).reshape(depth, 8, 6, d)

    def mod_vecs(layer, rows, g, which):
        shift, scale = mod[layer, rows, 3 * which], mod[layer, rows, 3 * which + 1]
        return (g[None, :] * (1.0 + scale))[:, None, :], shift[:, None, :]

    def gate_vec(layer, rows, which):
        return mod[layer, rows, 3 * which + 2][:, None, :]

    lat = slice(0, b)
    w_in = gm_w_in[0]
    q_off, k_off, v_off, r_off = 0, hk, 2 * hk, 2 * hk + hv
    zf_off = r_off + hv
    f_off = zf_off + 2 * GATE_RANK
    wzr, wzi = _fold_dft_call(w_in[:, f_off:], chan)
    w_main = jnp.concatenate([w_in[:, :zf_off].astype(BF16), wzr, wzi], axis=-1)
    w_z = jnp.pad(w_in[:, zf_off:f_off], ((0, 0), (0, Z_PAD - 2 * GATE_RANK))).astype(BF16)

    def gate_w(gw, off):
        return jnp.pad(gw, ((off, Z_PAD - GATE_RANK - off), (0, 0))).astype(BF16)

    gw_f, gw_b = gate_w(gla_gate_w_fwd[0], 0), gate_w(gla_gate_w_bwd[0], GATE_RANK)
    gb_f, gb_b = gla_gate_b_fwd[0][None, :], gla_gate_b_bwd[0][None, :]

    gs, sh = mod_vecs(0, lat, norm_mix_g[0], 0)
    gs_c, sh_c = mod_vecs(0, slice(b, b + 1), norm_mix_g[0], 0)
    gs_c, sh_c = jnp.broadcast_to(gs_c, (b, 1, d)), jnp.broadcast_to(sh_c, (b, 1, d))

    pc, _, zc = _inproj_call(ctx, None, gs_c, sh_c, w_main, w_z, ctx.shape[1], 512, zf_off)
    p, zf, z = _inproj_call(x, (per, pec), gs, sh, w_main, w_z, tm_big, 1024, zf_off)

    s0_f = _ctx_state_call(pc, zc, gw_f, gb_f, True)
    s0_b = _ctx_state_call(pc, zc, gw_b, gb_b, False)
    ob = _gla_call(p, z, gw_b, gb_b, s0_b, False)
    og = _gla_call(p, z, gw_f, gb_f, s0_f, True, ob=ob, ng=gla_norm_g[0][None, :])

    yf = _fourier_call(zf, w1, m2, n1, n2)

    h, a_ffn = _outproj_call(og, yf, gm_w_out[0].astype(BF16), x, per, pec, gate_vec(0, lat, 0),
                             *mod_vecs(0, lat, norm_ffn_g[0], 1), tm_mid)
    hid = _pair_call(a_ffn, ffn_w_in[0].astype(BF16), None, tm_huge, 512, "ffn_swiglu_in")
    h, a_mix = _ffn_out_call(hid, ffn_w_out[0].astype(BF16), h, gate_vec(0, lat, 1), None,
                             mod_vecs(1, lat, norm_mix_g[1], 0), tm_small)

    u = _pair_call(a_mix, cv_w_pw1[0].astype(BF16), cv_b_pw1[0][None, :], tm_huge, 512, "pointwise1_glu")
    wdw = jnp.pad(cv_w_dw[0], ((1, 0), (0, 0)))
    h, a_ffn = _conv_pw2_call(u, wdw, cv_b_dw[0][None, :], cv_ln_g[0][None, :], cv_ln_b[0][None, :],
                              cv_w_pw2[0].astype(BF16), cv_b_pw2[0][None, :], h, gate_vec(1, lat, 0),
                              *mod_vecs(1, lat, norm_ffn_g[1], 1), tm_mid)
    hid = _pair_call(a_ffn, ffn_w_in[1].astype(BF16), None, tm_huge, 512, "ffn_swiglu_in")
    return _ffn_out_call(hid, ffn_w_out[1].astype(BF16), h, gate_vec(1, lat, 1), final_norm_g[None, :], None, tm_small)
```

```python
import functools
import math

import jax
import jax.numpy as jnp
from jax import lax
from jax.experimental import pallas as pl
from jax.experimental.pallas import tpu as pltpu

F32 = jnp.float32
BF16 = jnp.bfloat16

EPS = 1e-6
GRID_W = 64
GLA_HEADS = 4
GLA_DK = 128
GLA_DV = 256
GATE_RANK = 16
GATE_NORM = 16.0
CHUNK = 64
GLA_SUBCHUNKS = 2
FN_GROUPS = 4
FN_GROUP_W = 256
FFT_COLS = 8
CONV_W = 31
CONV_HALO = 16
ROWS = 64
CONV_WIN = ROWS + 8 * (CONV_W // 8)
Z_PAD = 128

_MIB = 1024 * 1024


def _cparams(sem, vmem_mib):
    return pltpu.CompilerParams(dimension_semantics=sem, vmem_limit_bytes=vmem_mib * _MIB)


def _dot(a, b):
    return jnp.dot(a, b, preferred_element_type=F32)


def _dot_nt(a, b):
    return lax.dot_general(a, b, (((1,), (1,)), ((), ())), preferred_element_type=F32)


def _dot_tn(a, b):
    return lax.dot_general(a, b, (((0,), (0,)), ((), ())), preferred_element_type=F32)


def _silu(x):
    return x * jax.nn.sigmoid(x)


def _ada_kernel(c_ref, w_ref, b_ref, o_ref):
    s = _silu(c_ref[...]).astype(BF16)
    o_ref[...] = _dot(s, w_ref[...].astype(BF16)) + b_ref[...]


def _ada_call(cin, ada_w, ada_b):
    depth, d, n = ada_w.shape
    tn = 1024
    return pl.pallas_call(
        _ada_kernel,
        grid=(depth, n // tn),
        in_specs=[pl.BlockSpec((8, d), lambda l, j: (0, 0)),
                  pl.BlockSpec((None, d, tn), lambda l, j: (l, 0, j)),
                  pl.BlockSpec((None, 1, tn), lambda l, j: (l, 0, j))],
        out_specs=pl.BlockSpec((None, 8, tn), lambda l, j: (l, 0, j)),
        out_shape=jax.ShapeDtypeStruct((depth, 8, n), F32),
        compiler_params=_cparams(("parallel", "parallel"), 40),
        name="ada_modulation",
    )(cin, ada_w, ada_b.reshape(depth, 1, n))


def _split_bf16(v):
    hi = v.astype(BF16)
    lo = (v - hi.astype(F32)).astype(BF16)
    return hi, lo


def _fold_dft_kernel(wf_ref, cs_ref, zr_ref, zi_ref):
    w_hi, w_lo = _split_bf16(wf_ref[...])
    c_hi, c_lo = _split_bf16(cs_ref[...])
    r = _dot(w_hi, c_hi) + _dot(w_hi, c_lo) + _dot(w_lo, c_hi)
    zr_ref[...] = r[:, :FN_GROUP_W].astype(BF16)
    zi_ref[...] = r[:, FN_GROUP_W:].astype(BF16)


def _fold_dft_call(wf, cs):
    d = wf.shape[0]
    gw = FN_GROUP_W
    out = jax.ShapeDtypeStruct((d, FN_GROUPS * gw), BF16)
    return pl.pallas_call(
        _fold_dft_kernel,
        grid=(FN_GROUPS,),
        in_specs=[pl.BlockSpec((d, gw), lambda g: (0, g)),
                  pl.BlockSpec((gw, 2 * gw), lambda g: (0, 0))],
        out_specs=[pl.BlockSpec((d, gw), lambda g: (0, g)),
                   pl.BlockSpec((d, gw), lambda g: (0, g))],
        out_shape=[out, out],
        compiler_params=_cparams(("parallel",), 32),
        name="fold_channel_dft",
    )(wf, cs)


def _rms_mod(h, gs, sh):
    return (h * lax.rsqrt(jnp.mean(h * h, axis=-1, keepdims=True) + EPS) * gs + sh).astype(BF16)


def _rms_mod_prologue(x_ref, pe_refs, gs_ref, sh_ref, a_sc, tm):
    gs = gs_ref[...]
    sh = sh_ref[...]
    half = x_ref.shape[-1] // 2

    def body(r, carry):
        r0 = pl.multiple_of(r * ROWS, ROWS)
        h = x_ref[pl.ds(r0, ROWS), :]
        if pe_refs is not None:
            per_ref, pec_ref = pe_refs
            row = jnp.broadcast_to(per_ref[pl.ds(r, 1), :], (ROWS, half))
            h = h + jnp.concatenate([row, pec_ref[...]], axis=-1)
        a_sc[pl.ds(r0, ROWS), :] = _rms_mod(h, gs, sh)
        return carry

    lax.fori_loop(0, tm // ROWS, body, 0)


def _inproj_kernel(*refs, tm, with_pe, n_main):
    if with_pe:
        x_ref, per_ref, pec_ref, gs_ref, sh_ref, w_ref, wz_ref, p_ref, zf_ref, z_ref, a_sc = refs
        pe_refs = (per_ref, pec_ref)
    else:
        x_ref, gs_ref, sh_ref, w_ref, wz_ref, p_ref, zf_ref, z_ref, a_sc = refs
        pe_refs = None
    j = pl.program_id(2)

    @pl.when(j == 0)
    def _():
        _rms_mod_prologue(x_ref, pe_refs, gs_ref, sh_ref, a_sc, tm)
        z_ref[...] = _dot(a_sc[...], wz_ref[...])

    y = _dot(a_sc[...], w_ref[...])

    @pl.when(j < n_main)
    def _():
        p_ref[...] = y.astype(BF16)

    @pl.when(j >= n_main)
    def _():
        zf_ref[...] = y


def _inproj_call(x, pe, gs, sh, w, wz, tm, tn, n_main_cols):
    b, l, d = x.shape
    n = w.shape[1]
    n_main = n_main_cols // tn
    with_pe = pe is not None
    in_specs = [pl.BlockSpec((None, tm, d), lambda bi, i, j: (bi, i, 0))]
    args = [x]
    if with_pe:
        per, pec = pe
        in_specs += [pl.BlockSpec((tm // GRID_W, d // 2), lambda bi, i, j: (i, 0)),
                     pl.BlockSpec((GRID_W, d // 2), lambda bi, i, j: (0, 0))]
        args += [per, pec]
    in_specs += [pl.BlockSpec((None, 1, d), lambda bi, i, j: (bi, 0, 0)),
                 pl.BlockSpec((None, 1, d), lambda bi, i, j: (bi, 0, 0)),
                 pl.BlockSpec((d, tn), lambda bi, i, j: (0, j)),
                 pl.BlockSpec((d, Z_PAD), lambda bi, i, j: (0, 0))]
    args += [gs, sh, w, wz]
    return pl.pallas_call(
        functools.partial(_inproj_kernel, tm=tm, with_pe=with_pe, n_main=n_main),
        grid=(b, l // tm, n // tn),
        in_specs=in_specs,
        out_specs=[pl.BlockSpec((None, tm, tn), lambda bi, i, j: (bi, i, jnp.minimum(j, n_main - 1))),
                   pl.BlockSpec((None, tm, tn), lambda bi, i, j: (bi, i, jnp.maximum(j - n_main, 0))),
                   pl.BlockSpec((None, tm, Z_PAD), lambda bi, i, j: (bi, i, 0))],
        out_shape=[jax.ShapeDtypeStruct((b, l, n_main_cols), BF16),
                   jax.ShapeDtypeStruct((b, l, n - n_main_cols), F32),
                   jax.ShapeDtypeStruct((b, l, Z_PAD), F32)],
        scratch_shapes=[pltpu.VMEM((tm, d), BF16)],
        compiler_params=_cparams(("parallel", "parallel", "arbitrary"), 56),
        name="input_projection_pe" if with_pe else "input_projection_ctx",
    )(*args)


def _tri_mask(rows, n, fwd):
    row = lax.broadcasted_iota(jnp.int32, (rows, rows), 0)
    col = lax.broadcasted_iota(jnp.int32, (rows, rows), 1)
    order = (col <= row) if fwd else (col >= row)
    if rows == n:
        return order
    return order & ((row // n) == (col // n))


def _gate_cum(z, gw_ref, gb_ref, mask):
    logit = _dot(z.astype(BF16), gw_ref[...]) + gb_ref[...]
    g = (jnp.minimum(logit, 0.0) - jnp.log1p(jnp.exp(-jnp.abs(logit)))) * (1.0 / GATE_NORM)
    tri = jnp.where(mask, 1.0, 0.0).astype(BF16)
    g_hi, g_lo = _split_bf16(g)
    return _dot(tri, g_hi) + _dot(tri, g_lo)


def _gla_kernel(*refs, fwd, n, nsub, nb):
    if fwd:
        (q_ref, k_ref, v_ref, z_ref, gw_ref, gb_ref, s0_ref, ob_ref, r_ref, ng_ref, o_ref, st_sc) = refs
    else:
        (q_ref, k_ref, v_ref, z_ref, gw_ref, gb_ref, s0_ref, o_ref, st_sc) = refs

    @pl.when(pl.program_id(0) == 0)
    def _():
        st_sc[...] = s0_ref[...]

    mask = _tri_mask(n, n, fwd)
    cmask = _tri_mask(nsub * n, n, fwd)
    last = n - 1 if fwd else 0
    scale = GLA_DK ** -0.5
    for bi in range(nb):
        b_all = _gate_cum(z_ref[bi], gw_ref, gb_ref, cmask)
        for cc in (range(nsub) if fwd else reversed(range(nsub))):
            rows = slice(cc * n, (cc + 1) * n)
            for h in range(GLA_HEADS):
                ks = slice(h * GLA_DK, (h + 1) * GLA_DK)
                vs = slice(h * GLA_DV, (h + 1) * GLA_DV)
                bh = b_all[rows, ks]
                bl = bh[last:last + 1, :]
                q = q_ref[bi, rows, ks].astype(F32) * scale
                k = k_ref[bi, rows, ks].astype(F32)
                v = v_ref[bi, rows, vs]
                qd = (q * jnp.exp(bh)).astype(BF16)
                ki = (k * jnp.exp(-bh)).astype(BF16)
                kt = (k * jnp.exp(bl - bh)).astype(BF16)
                att = jnp.where(mask, _dot_nt(qd, ki), 0.0).astype(BF16)
                st = st_sc[bi, h]
                o = _dot(att, v) + _dot_nt(qd, st.astype(BF16))
                st_sc[bi, h] = jnp.exp(bl) * st + _dot_tn(v, kt)
                if fwd:
                    o = o + ob_ref[bi, rows, vs]
                    on = o * lax.rsqrt(jnp.mean(o * o, axis=-1, keepdims=True) + EPS) * ng_ref[...]
                    o_ref[bi, rows, vs] = (on * _silu(r_ref[bi, rows, vs].astype(F32))).astype(BF16)
                else:
                    o_ref[bi, rows, vs] = o


def _gla_call(p, z, gw, gb, s0, fwd, ob=None, ng=None):
    b, l, _ = p.shape
    n = CHUNK
    nsub = GLA_SUBCHUNKS
    rows = n * nsub
    ns = l // rows
    hk = GLA_HEADS * GLA_DK
    hv = GLA_HEADS * GLA_DV
    cidx = (lambda c: c) if fwd else (lambda c: ns - 1 - c)
    in_specs = [pl.BlockSpec((b, rows, hk), lambda c: (0, cidx(c), 0)),
                pl.BlockSpec((b, rows, hk), lambda c: (0, cidx(c), 1)),
                pl.BlockSpec((b, rows, hv), lambda c: (0, cidx(c), 1)),
                pl.BlockSpec((b, rows, Z_PAD), lambda c: (0, cidx(c), 0)),
                pl.BlockSpec((Z_PAD, hk), lambda c: (0, 0)),
                pl.BlockSpec((1, hk), lambda c: (0, 0)),
                pl.BlockSpec((b, GLA_HEADS, GLA_DV, GLA_DK), lambda c: (0, 0, 0, 0))]
    args = [p, p, p, z, gw, gb, s0]
    if fwd:
        in_specs += [pl.BlockSpec((b, rows, hv), lambda c: (0, cidx(c), 0)),
                     pl.BlockSpec((b, rows, hv), lambda c: (0, cidx(c), 2)),
                     pl.BlockSpec((1, GLA_DV), lambda c: (0, 0))]
        args += [ob, p, ng]
    return pl.pallas_call(
        functools.partial(_gla_kernel, fwd=fwd, n=n, nsub=nsub, nb=b),
        grid=(ns,),
        in_specs=in_specs,
        out_specs=pl.BlockSpec((b, rows, hv), lambda c: (0, cidx(c), 0)),
        out_shape=jax.ShapeDtypeStruct((b, l, hv), BF16 if fwd else F32),
        scratch_shapes=[pltpu.VMEM((b, GLA_HEADS, GLA_DV, GLA_DK), F32)],
        compiler_params=_cparams(("arbitrary",), 32),
        name="gla_scan_fwd" if fwd else "gla_scan_bwd",
    )(*args)


def _ctx_state_kernel(k_ref, v_ref, z_ref, gw_ref, gb_ref, s_ref, *, fwd, n):
    mask = _tri_mask(n, n, fwd)
    b = _gate_cum(z_ref[...], gw_ref, gb_ref, mask)
    last = n - 1 if fwd else 0
    for h in range(GLA_HEADS):
        ks = slice(h * GLA_DK, (h + 1) * GLA_DK)
        vs = slice(h * GLA_DV, (h + 1) * GLA_DV)
        bh = b[:, ks]
        kt = (k_ref[:, ks].astype(F32) * jnp.exp(bh[last:last + 1, :] - bh)).astype(BF16)
        s_ref[h] = _dot_tn(v_ref[:, vs], kt)


def _ctx_state_call(pc, zc, gw, gb, fwd):
    b, n, _ = pc.shape
    hk = GLA_HEADS * GLA_DK
    hv = GLA_HEADS * GLA_DV
    return pl.pallas_call(
        functools.partial(_ctx_state_kernel, fwd=fwd, n=n),
        grid=(b,),
        in_specs=[pl.BlockSpec((None, n, hk), lambda bi: (bi, 0, 1)),
                  pl.BlockSpec((None, n, hv), lambda bi: (bi, 0, 1)),
                  pl.BlockSpec((None, n, Z_PAD), lambda bi: (bi, 0, 0)),
                  pl.BlockSpec((Z_PAD, hk), lambda bi: (0, 0)),
                  pl.BlockSpec((1, hk), lambda bi: (0, 0))],
        out_specs=pl.BlockSpec((None, GLA_HEADS, GLA_DV, GLA_DK), lambda bi: (bi, 0, 0, 0)),
        out_shape=jax.ShapeDtypeStruct((b, GLA_HEADS, GLA_DV, GLA_DK), F32),
        compiler_params=_cparams(("parallel",), 32),
        name="ctx_state_fwd" if fwd else "ctx_state_bwd",
    )(pc, pc, zc, gw, gb)


def _fft1_kernel(z_ref, w_ref, o_ref, *, n1, cw):
    for j in range(FFT_COLS):
        zj = z_ref[:, j, :]
        zz = jnp.concatenate([zj[:, :cw], zj[:, cw:]], axis=0).astype(BF16)
        a = _dot(w_ref[...], zz)
        o_ref[0, j] = a[:n1]
        o_ref[1, j] = a[n1:]


def _fft2_kernel(a_ref, m_ref, o_ref):
    for j in range(FFT_COLS):
        aa = jnp.concatenate([a_ref[0, :, j, :], a_ref[1, :, j, :]], axis=0).astype(BF16)
        o_ref[:, j, :] = _dot(m_ref[j], aa)


def _fourier_call(zf, w1, m2, n1, n2):
    b, l, cw2 = zf.shape
    cw = cw2 // 2
    fc = FFT_COLS
    a = pl.pallas_call(
        functools.partial(_fft1_kernel, n1=n1, cw=cw),
        grid=(b, n2 // fc),
        in_specs=[pl.BlockSpec((None, n1, fc, cw2), lambda bi, s: (bi, 0, s, 0)),
                  pl.BlockSpec((2 * n1, 2 * n1), lambda bi, s: (0, 0))],
        out_specs=pl.BlockSpec((None, 2, fc, n1, cw), lambda bi, s: (bi, 0, s, 0, 0)),
        out_shape=jax.ShapeDtypeStruct((b, 2, n2, n1, cw), F32),
        compiler_params=_cparams(("parallel", "parallel"), 48),
        name="position_dft_stage1",
    )(zf.reshape(b, n1, n2, cw2), w1)
    y = pl.pallas_call(
        _fft2_kernel,
        grid=(b, n1 // fc),
        in_specs=[pl.BlockSpec((None, 2, n2, fc, cw), lambda bi, s: (bi, 0, 0, s, 0)),
                  pl.BlockSpec((fc, n2, 2 * n2), lambda bi, s: (s, 0, 0))],
        out_specs=pl.BlockSpec((None, n2, fc, cw), lambda bi, s: (bi, 0, s, 0)),
        out_shape=jax.ShapeDtypeStruct((b, n2, n1, cw), F32),
        compiler_params=_cparams(("parallel", "parallel"), 48),
        name="position_dft_stage2",
    )(a, m2)
    return y.reshape(b, l, cw)


def _outproj_kernel(x1_ref, x2_ref, w_ref, xr_ref, per_ref, pec_ref, gate_ref, gsn_ref, shn_ref, o_ref, an_ref, *, tm):
    kh = x1_ref.shape[-1]
    half = o_ref.shape[-1] // 2
    y = _dot(x1_ref[...], w_ref[:kh, :]) + _dot(x2_ref[...].astype(BF16), w_ref[kh:, :])
    gate = gate_ref[...]
    for r in range(tm // ROWS):
        rows = slice(r * ROWS, (r + 1) * ROWS)
        pe = jnp.concatenate([jnp.broadcast_to(per_ref[r:r + 1, :], (ROWS, half)), pec_ref[...]], axis=-1)
        h = xr_ref[rows, :] + pe + gate * y[rows, :]
        o_ref[rows, :] = h
        an_ref[rows, :] = _rms_mod(h, gsn_ref[...], shn_ref[...])


def _outproj_call(x1, x2, w, x, per, pec, gate, gsn, shn, tm):
    b, l, d = x.shape
    kh = x1.shape[-1]
    half = d // 2
    bvec = pl.BlockSpec((None, 1, d), lambda bi, i: (bi, 0, 0))
    return pl.pallas_call(
        functools.partial(_outproj_kernel, tm=tm),
        grid=(b, l // tm),
        in_specs=[pl.BlockSpec((None, tm, kh), lambda bi, i: (bi, i, 0)),
                  pl.BlockSpec((None, tm, kh), lambda bi, i: (bi, i, 0)),
                  pl.BlockSpec((2 * kh, d), lambda bi, i: (0, 0), pipeline_mode=pl.Buffered(1)),
                  pl.BlockSpec((None, tm, d), lambda bi, i: (bi, i, 0)),
                  pl.BlockSpec((tm // GRID_W, half), lambda bi, i: (i, 0)),
                  pl.BlockSpec((GRID_W, half), lambda bi, i: (0, 0)),
                  bvec, bvec, bvec],
        out_specs=[pl.BlockSpec((None, tm, d), lambda bi, i: (bi, i, 0)),
                   pl.BlockSpec((None, tm, d), lambda bi, i: (bi, i, 0))],
        out_shape=[jax.ShapeDtypeStruct((b, l, d), F32), jax.ShapeDtypeStruct((b, l, d), BF16)],
        compiler_params=_cparams(("parallel", "parallel"), 56),
        name="output_projection",
    )(x1, x2, w, x, per, pec, gate, gsn, shn)


def _pair_kernel(*refs, glu):
    if glu:
        a_ref, wa_ref, wb_ref, ba_ref, bb_ref, o_ref = refs
    else:
        a_ref, wa_ref, wb_ref, o_ref = refs
    a = a_ref[...]
    ya = _dot(a, wa_ref[...])
    yb = _dot(a, wb_ref[...])
    if glu:
        o_ref[...] = ((ya + ba_ref[...]) * jax.nn.sigmoid(yb + bb_ref[...])).astype(BF16)
    else:
        o_ref[...] = (_silu(ya) * yb).astype(BF16)


def _pair_call(a, w, layer, bias, tm, tn, name):
    b, l, d = a.shape
    nh = w.shape[2] // 2
    nj = nh // tn
    glu = bias is not None
    in_specs = [pl.BlockSpec((None, tm, d), lambda bi, i, j: (bi, i, 0)),
                pl.BlockSpec((None, d, tn), lambda bi, i, j: (layer, 0, j)),
                pl.BlockSpec((None, d, tn), lambda bi, i, j: (layer, 0, j + nj))]
    args = [a, w, w]
    if glu:
        in_specs += [pl.BlockSpec((1, tn), lambda bi, i, j: (0, j)),
                     pl.BlockSpec((1, tn), lambda bi, i, j: (0, j + nj))]
        args += [bias, bias]
    return pl.pallas_call(
        functools.partial(_pair_kernel, glu=glu),
        grid=(b, l // tm, nj),
        in_specs=in_specs,
        out_specs=pl.BlockSpec((None, tm, tn), lambda bi, i, j: (bi, i, j)),
        out_shape=jax.ShapeDtypeStruct((b, l, nh), BF16),
        compiler_params=_cparams(("parallel", "parallel", "arbitrary"), 48),
        name=name,
    )(*args)


def _ffn_out_kernel(*refs, final):
    if final:
        x_ref, w_ref, res_ref, gate_ref, g_ref, o_ref = refs
    else:
        x_ref, w_ref, res_ref, gate_ref, gsn_ref, shn_ref, o_ref, an_ref = refs
    y = res_ref[...] + gate_ref[...] * _dot(x_ref[...], w_ref[...])
    if final:
        o_ref[...] = y * lax.rsqrt(jnp.mean(y * y, axis=-1, keepdims=True) + EPS) * g_ref[...]
    else:
        o_ref[...] = y
        an_ref[...] = _rms_mod(y, gsn_ref[...], shn_ref[...])


def _ffn_out_call(xh, w, layer, res, gate, final_g, nxt, tm):
    b, l, kk = xh.shape
    d = w.shape[2]
    final = final_g is not None
    row = pl.BlockSpec((None, tm, d), lambda bi, i: (bi, i, 0))
    bvec = pl.BlockSpec((None, 1, d), lambda bi, i: (bi, 0, 0))
    in_specs = [pl.BlockSpec((None, tm, kk), lambda bi, i: (bi, i, 0)),
                pl.BlockSpec((None, kk, d), lambda bi, i: (layer, 0, 0), pipeline_mode=pl.Buffered(1)),
                row, bvec]
    args = [xh, w, res, gate]
    if final:
        in_specs.append(pl.BlockSpec((1, d), lambda bi, i: (0, 0)))
        args.append(final_g)
        out_specs, out_shape = row, jax.ShapeDtypeStruct((b, l, d), F32)
    else:
        in_specs += [bvec, bvec]
        args += list(nxt)
        out_specs = [row, row]
        out_shape = [jax.ShapeDtypeStruct((b, l, d), F32), jax.ShapeDtypeStruct((b, l, d), BF16)]
    return pl.pallas_call(
        functools.partial(_ffn_out_kernel, final=final),
        grid=(b, l // tm),
        in_specs=in_specs,
        out_specs=out_specs,
        out_shape=out_shape,
        compiler_params=_cparams(("parallel", "parallel"), 56),
        name="ffn_output_final_norm" if final else "ffn_output_projection",
    )(*args)


def _conv_shift_matrix():
    rr = jnp.arange(7 * CONV_WIN)[:, None]
    return (jnp.arange(CONV_WIN + 8)[None, :] == (rr % CONV_WIN) + (rr // CONV_WIN) + 1).astype(BF16)


def _conv_ln_chunk(r, ext_sc, shm_ref, wdw_ref, bdw_ref, lng_ref, lnb_ref, cv_sc, a_dst):
    d = cv_sc.shape[-1]
    win = CONV_WIN
    lanes = 256
    r0 = pl.multiple_of(r * ROWS, ROWS)
    for cb in range(d // lanes):
        cs = slice(cb * lanes, (cb + 1) * lanes)
        e = ext_sc[pl.ds(r0, win + 8), cs]
        xs = _dot(shm_ref[...], e.astype(BF16))
        acc = None
        for s in range(8):
            for q in range(win // 8 - ROWS // 8 + 1):
                m = 8 * q + s
                if m == 0 or m > CONV_W:
                    continue
                lo = 8 * q if s == 0 else (s - 1) * win + 8 * q
                term = wdw_ref[m:m + 1, cs] * (e if s == 0 else xs)[lo:lo + ROWS, :]
                acc = term if acc is None else acc + term
        cv_sc[pl.ds(r0, ROWS), cs] = acc + bdw_ref[:, cs]
    c = cv_sc[pl.ds(r0, ROWS), :]
    xc = c - jnp.mean(c, axis=-1, keepdims=True)
    y = xc * lax.rsqrt(jnp.mean(xc * xc, axis=-1, keepdims=True) + EPS) * lng_ref[...] + lnb_ref[...]
    a_dst[pl.ds(r0, ROWS), :] = _silu(y).astype(BF16)


def _conv_pw2_kernel(um_ref, up_ref, un_ref, shm_ref, wdw_ref, bdw_ref, lng_ref, lnb_ref, w_ref, b2_ref, res_ref,
                     gate_ref, gsn_ref, shn_ref, o_ref, an_ref, ext_sc, cv_sc, a_sc, *, tm, nt):
    i = pl.program_id(1)
    ci = jnp.minimum(i, nt - 1)
    halo = CONV_HALO

    @pl.when(i == 0)
    def _():
        a_sc[1] = jnp.zeros(a_sc.shape[1:], BF16)

    ext_sc[0:halo, :] = jnp.where(ci > 0, up_ref[...].astype(F32), 0.0)
    ext_sc[halo:halo + tm, :] = um_ref[...].astype(F32)
    ext_sc[halo + tm:, :] = jnp.where(ci < nt - 1, un_ref[...].astype(F32), 0.0)
    cur = i % 2
    prev = 1 - cur
    n_slices = tm // ROWS
    cols = w_ref.shape[1] // n_slices

    def body(n, carry):
        c0 = pl.multiple_of(n * cols, cols)
        y = _dot(a_sc[prev], w_ref[:, pl.ds(c0, cols)]) + b2_ref[:, pl.ds(c0, cols)]
        o_ref[:, pl.ds(c0, cols)] = res_ref[:, pl.ds(c0, cols)] + gate_ref[:, pl.ds(c0, cols)] * y
        _conv_ln_chunk(n, ext_sc, shm_ref, wdw_ref, bdw_ref, lng_ref, lnb_ref, cv_sc, a_sc.at[cur])
        return carry

    lax.fori_loop(0, n_slices, body, 0)

    def act(r, carry):
        r0 = pl.multiple_of(r * ROWS, ROWS)
        an_ref[pl.ds(r0, ROWS), :] = _rms_mod(o_ref[pl.ds(r0, ROWS), :], gsn_ref[...], shn_ref[...])
        return carry

    lax.fori_loop(0, tm // ROWS, act, 0)


def _conv_pw2_call(u, wdw, bdw, lng, lnb, w, b2, res, gate, gsn, shn, tm):
    b, l, d = u.shape
    halo = CONV_HALO
    nt = l // tm
    hb = tm // halo
    nhb = l // halo
    shm = _conv_shift_matrix()

    def tile(i):
        return jnp.minimum(i, nt - 1)

    def prev(i):
        return jnp.maximum(i - 1, 0)

    vec = pl.BlockSpec((1, d), lambda bi, i: (0, 0))
    bvec = pl.BlockSpec((None, 1, d), lambda bi, i: (bi, 0, 0))
    return pl.pallas_call(
        functools.partial(_conv_pw2_kernel, tm=tm, nt=nt),
        grid=(b, nt + 1),
        in_specs=[pl.BlockSpec((None, tm, d), lambda bi, i: (bi, tile(i), 0)),
                  pl.BlockSpec((None, halo, d), lambda bi, i: (bi, jnp.maximum(tile(i) * hb - 1, 0), 0)),
                  pl.BlockSpec((None, halo, d), lambda bi, i: (bi, jnp.minimum((tile(i) + 1) * hb, nhb - 1), 0)),
                  pl.BlockSpec(shm.shape, lambda bi, i: (0, 0)),
                  pl.BlockSpec((CONV_W + 1, d), lambda bi, i: (0, 0)),
                  vec, vec, vec,
                  pl.BlockSpec((None, d, d), lambda bi, i: (0, 0, 0), pipeline_mode=pl.Buffered(1)),
                  vec,
                  pl.BlockSpec((None, tm, d), lambda bi, i: (bi, prev(i), 0)),
                  bvec, bvec, bvec],
        out_specs=[pl.BlockSpec((None, tm, d), lambda bi, i: (bi, prev(i), 0)),
                   pl.BlockSpec((None, tm, d), lambda bi, i: (bi, prev(i), 0))],
        out_shape=[jax.ShapeDtypeStruct((b, l, d), F32), jax.ShapeDtypeStruct((b, l, d), BF16)],
        scratch_shapes=[pltpu.VMEM((tm + 2 * halo, d), F32),
                        pltpu.VMEM((tm, d), F32),
                        pltpu.VMEM((2, tm, d), BF16)],
        compiler_params=_cparams(("parallel", "arbitrary"), 56),
        name="conv_ln_pointwise2",
    )(u, u, u, shm, wdw, bdw, lng, lnb, w, b2, res, gate, gsn, shn)


def _pos_tables(rows, d):
    nq = d // 4
    omega = 1.0 / (10000.0 ** (jnp.arange(nq, dtype=F32) / nq))
    ar = jnp.arange(rows, dtype=F32)[:, None] * omega[None, :]
    ac = jnp.arange(GRID_W, dtype=F32)[:, None] * omega[None, :]
    per = jnp.concatenate([jnp.sin(ar), jnp.cos(ar)], axis=-1)
    pec = jnp.concatenate([jnp.sin(ac), jnp.cos(ac)], axis=-1)
    return per, pec


def _cos_sin(num, den):
    ang = (num % den).astype(F32) * (2.0 * math.pi / den)
    return jnp.cos(ang), jnp.sin(ang)


def _dft_tables(n1, n2):
    i32 = jnp.int32
    gw = FN_GROUP_W
    cc, sc = _cos_sin(jnp.arange(gw, dtype=i32)[:, None] * jnp.arange(gw, dtype=i32)[None, :], gw)
    chan = jnp.concatenate([cc, -sc], axis=-1)
    c1, s1 = _cos_sin(jnp.arange(n1, dtype=i32)[:, None] * jnp.arange(n1, dtype=i32)[None, :], n1)
    w1 = jnp.concatenate([jnp.concatenate([c1, s1], axis=-1),
                          jnp.concatenate([-s1, c1], axis=-1)], axis=0).astype(BF16)
    cw_, sw_ = _cos_sin(jnp.arange(n2, dtype=i32)[:, None] * jnp.arange(n2, dtype=i32)[None, :], n2)
    ct_, st_ = _cos_sin(jnp.arange(n1, dtype=i32)[:, None] * jnp.arange(n2, dtype=i32)[None, :], n1 * n2)
    c2 = cw_[None, :, :] * ct_[:, None, :] - sw_[None, :, :] * st_[:, None, :]
    s2 = sw_[None, :, :] * ct_[:, None, :] + cw_[None, :, :] * st_[:, None, :]
    norm = 1.0 / math.sqrt(n1 * n2 * gw)
    m2 = (jnp.concatenate([c2, s2], axis=-1) * norm).astype(BF16)
    return chan, w1, m2


def _split_len(l):
    n1 = 1 << ((l.bit_length() - 1) // 2)
    n2 = l // n1
    assert n1 * n2 == l and n1 % 8 == 0 and n2 % 16 == 0, l
    return n1, n2


def kernel(x, c, ctx, c_ctx, ada_w, ada_b, norm_mix_g, norm_ffn_g, ffn_w_in, ffn_w_out, gm_w_in, gla_gate_w_fwd, gla_gate_b_fwd, gla_gate_w_bwd, gla_gate_b_bwd, gla_norm_g, gm_w_out, cv_w_pw1, cv_b_pw1, cv_w_dw, cv_b_dw, cv_ln_g, cv_ln_b, cv_w_pw2, cv_b_pw2, final_norm_g):
    b, l, d = x.shape
    depth = ada_w.shape[0]
    assert depth == 2 and b + 1 <= 8
    hk = GLA_HEADS * GLA_DK
    hv = GLA_HEADS * GLA_DV
    n1, n2 = _split_len(l)
    tm_huge = min(2048, l)
    tm_big = min(1024, l)
    tm_mid = min(512, l)
    tm_small = min(256, l)

    per, pec = _pos_tables(l // GRID_W, d)
    chan, w1, m2 = _dft_tables(n1, n2)

    cin = jnp.zeros((8, d), F32).at[:b].set(c).at[b].set(c_ctx)
    mod = _ada_call(cin, ada_w, ada_b).reshape(depth, 8, 6, d)

    def mod_vecs(layer, rows, g, which):
        shift, scale = mod[layer, rows, 3 * which], mod[layer, rows, 3 * which + 1]
        return (g[None, :] * (1.0 + scale))[:, None, :], shift[:, None, :]

    def gate_vec(layer, rows, which):
        return mod[layer, rows, 3 * which + 2][:, None, :]

    lat = slice(0, b)
    w_in = gm_w_in[0]
    q_off, k_off, v_off, r_off = 0, hk, 2 * hk, 2 * hk + hv
    zf_off = r_off + hv
    f_off = zf_off + 2 * GATE_RANK
    wzr, wzi = _fold_dft_call(w_in[:, f_off:], chan)
    w_main = jnp.concatenate([w_in[:, :zf_off].astype(BF16), wzr, wzi], axis=-1)
    w_z = jnp.pad(w_in[:, zf_off:f_off], ((0, 0), (0, Z_PAD - 2 * GATE_RANK))).astype(BF16)

    def gate_w(gw, off):
        return jnp.pad(gw, ((off, Z_PAD - GATE_RANK - off), (0, 0))).astype(BF16)

    gw_f, gw_b = gate_w(gla_gate_w_fwd[0], 0), gate_w(gla_gate_w_bwd[0], GATE_RANK)
    gb_f, gb_b = gla_gate_b_fwd[0][None, :], gla_gate_b_bwd[0][None, :]

    gs, sh = mod_vecs(0, lat, norm_mix_g[0], 0)
    gs_c, sh_c = mod_vecs(0, slice(b, b + 1), norm_mix_g[0], 0)
    gs_c, sh_c = jnp.broadcast_to(gs_c, (b, 1, d)), jnp.broadcast_to(sh_c, (b, 1, d))

    pc, _, zc = _inproj_call(ctx, None, gs_c, sh_c, w_main, w_z, ctx.shape[1], 512, zf_off)
    p, zf, z = _inproj_call(x, (per, pec), gs, sh, w_main, w_z, tm_big, 1024, zf_off)

    s0_f = _ctx_state_call(pc, zc, gw_f, gb_f, True)
    s0_b = _ctx_state_call(pc, zc, gw_b, gb_b, False)
    ob = _gla_call(p, z, gw_b, gb_b, s0_b, False)
    og = _gla_call(p, z, gw_f, gb_f, s0_f, True, ob=ob, ng=gla_norm_g[0][None, :])

    yf = _fourier_call(zf, w1, m2, n1, n2)

    w_ffn_in, w_ffn_out = ffn_w_in.astype(BF16), ffn_w_out.astype(BF16)
    h, a_ffn = _outproj_call(og, yf, gm_w_out[0].astype(BF16), x, per, pec, gate_vec(0, lat, 0),
                             *mod_vecs(0, lat, norm_ffn_g[0], 1), tm_mid)
    hid = _pair_call(a_ffn, w_ffn_in, 0, None, tm_huge, 512, "ffn_swiglu_in")
    h, a_mix = _ffn_out_call(hid, w_ffn_out, 0, h, gate_vec(0, lat, 1), None,
                             mod_vecs(1, lat, norm_mix_g[1], 0), tm_small)

    u = _pair_call(a_mix, cv_w_pw1.astype(BF16), 0, cv_b_pw1[0][None, :], tm_huge, 512, "pointwise1_glu")
    wdw = jnp.pad(cv_w_dw[0], ((1, 0), (0, 0)))
    h, a_ffn = _conv_pw2_call(u, wdw, cv_b_dw[0][None, :], cv_ln_g[0][None, :], cv_ln_b[0][None, :],
                              cv_w_pw2.astype(BF16), cv_b_pw2[0][None, :], h, gate_vec(1, lat, 0),
                              *mod_vecs(1, lat, norm_ffn_g[1], 1), tm_mid)
    hid = _pair_call(a_ffn, w_ffn_in, 1, None, tm_huge, 512, "ffn_swiglu_in")
    return _ffn_out_call(hid, w_ffn_out, 1, h, gate_vec(1, lat, 1), final_norm_g[None, :], None, tm_small)
```

```python
import functools
import math

import jax
import jax.numpy as jnp
from jax import lax
from jax.experimental import pallas as pl
from jax.experimental.pallas import tpu as pltpu

F32 = jnp.float32
BF16 = jnp.bfloat16

EPS = 1e-6
GRID_W = 64
GLA_HEADS = 4
GLA_DK = 128
GLA_DV = 256
GATE_RANK = 16
GATE_NORM = 16.0
CHUNK = 64
GLA_SUBCHUNKS = 4
FN_GROUPS = 4
FN_GROUP_W = 256
FFT_COLS = 8
CONV_W = 31
CONV_HALO = 16
ROWS = 64
CONV_WIN = ROWS + 8 * (CONV_W // 8)
Z_PAD = 128
PAIR_SPLIT = 4

_MIB = 1024 * 1024


def _cparams(sem, vmem_mib):
    return pltpu.CompilerParams(dimension_semantics=sem, vmem_limit_bytes=vmem_mib * _MIB)


def _dot(a, b):
    return jnp.dot(a, b, preferred_element_type=F32)


def _dot_nt(a, b):
    return lax.dot_general(a, b, (((1,), (1,)), ((), ())), preferred_element_type=F32)


def _dot_tn(a, b):
    return lax.dot_general(a, b, (((0,), (0,)), ((), ())), preferred_element_type=F32)


def _silu(x):
    return x * jax.nn.sigmoid(x)


def _ada_kernel(c_ref, w_ref, b_ref, o_ref):
    s = _silu(c_ref[...]).astype(BF16)
    o_ref[...] = _dot(s, w_ref[...].astype(BF16)) + b_ref[...]


def _ada_call(cin, ada_w, ada_b):
    depth, d, n = ada_w.shape
    tn = 1024
    return pl.pallas_call(
        _ada_kernel,
        grid=(depth, n // tn),
        in_specs=[pl.BlockSpec((8, d), lambda l, j: (0, 0)),
                  pl.BlockSpec((None, d, tn), lambda l, j: (l, 0, j)),
                  pl.BlockSpec((None, 1, tn), lambda l, j: (l, 0, j))],
        out_specs=pl.BlockSpec((None, 8, tn), lambda l, j: (l, 0, j)),
        out_shape=jax.ShapeDtypeStruct((depth, 8, n), F32),
        compiler_params=_cparams(("parallel", "parallel"), 40),
        name="ada_modulation",
    )(cin, ada_w, ada_b.reshape(depth, 1, n))


def _split_bf16(v):
    hi = v.astype(BF16)
    lo = (v - hi.astype(F32)).astype(BF16)
    return hi, lo


def _fold_dft_kernel(wf_ref, cs_ref, zr_ref, zi_ref):
    w_hi, w_lo = _split_bf16(wf_ref[...])
    c_hi, c_lo = _split_bf16(cs_ref[...])
    r = _dot(w_hi, c_hi) + _dot(w_hi, c_lo) + _dot(w_lo, c_hi)
    zr_ref[...] = r[:, :FN_GROUP_W].astype(BF16)
    zi_ref[...] = r[:, FN_GROUP_W:].astype(BF16)


def _fold_dft_call(wf, cs):
    d = wf.shape[0]
    gw = FN_GROUP_W
    out = jax.ShapeDtypeStruct((d, FN_GROUPS * gw), BF16)
    return pl.pallas_call(
        _fold_dft_kernel,
        grid=(FN_GROUPS,),
        in_specs=[pl.BlockSpec((d, gw), lambda g: (0, g)),
                  pl.BlockSpec((gw, 2 * gw), lambda g: (0, 0))],
        out_specs=[pl.BlockSpec((d, gw), lambda g: (0, g)),
                   pl.BlockSpec((d, gw), lambda g: (0, g))],
        out_shape=[out, out],
        compiler_params=_cparams(("parallel",), 32),
        name="fold_channel_dft",
    )(wf, cs)


def _rms_mod(h, gs, sh):
    return (h * lax.rsqrt(jnp.mean(h * h, axis=-1, keepdims=True) + EPS) * gs + sh).astype(BF16)


def _rms_mod_prologue(x_ref, pe_refs, gs_ref, sh_ref, a_sc, tm):
    gs = gs_ref[...]
    sh = sh_ref[...]
    half = x_ref.shape[-1] // 2

    def body(r, carry):
        r0 = pl.multiple_of(r * ROWS, ROWS)
        h = x_ref[pl.ds(r0, ROWS), :]
        if pe_refs is not None:
            per_ref, pec_ref = pe_refs
            row = jnp.broadcast_to(per_ref[pl.ds(r, 1), :], (ROWS, half))
            h = h + jnp.concatenate([row, pec_ref[...]], axis=-1)
        a_sc[pl.ds(r0, ROWS), :] = _rms_mod(h, gs, sh)
        return carry

    lax.fori_loop(0, tm // ROWS, body, 0)


def _inproj_kernel(*refs, tm, with_pe, n_main):
    if with_pe:
        x_ref, per_ref, pec_ref, gs_ref, sh_ref, w_ref, wz_ref, p_ref, zf_ref, z_ref, a_sc = refs
        pe_refs = (per_ref, pec_ref)
    else:
        x_ref, gs_ref, sh_ref, w_ref, wz_ref, p_ref, zf_ref, z_ref, a_sc = refs
        pe_refs = None
    j = pl.program_id(2)

    @pl.when(j == 0)
    def _():
        _rms_mod_prologue(x_ref, pe_refs, gs_ref, sh_ref, a_sc, tm)
        z_ref[...] = _dot(a_sc[...], wz_ref[...])

    y = _dot(a_sc[...], w_ref[...])

    @pl.when(j < n_main)
    def _():
        p_ref[...] = y.astype(BF16)

    @pl.when(j >= n_main)
    def _():
        zf_ref[...] = y


def _inproj_call(x, pe, gs, sh, w, wz, tm, tn, n_main_cols):
    b, l, d = x.shape
    n = w.shape[1]
    n_main = n_main_cols // tn
    with_pe = pe is not None
    in_specs = [pl.BlockSpec((None, tm, d), lambda bi, i, j: (bi, i, 0))]
    args = [x]
    if with_pe:
        per, pec = pe
        in_specs += [pl.BlockSpec((tm // GRID_W, d // 2), lambda bi, i, j: (i, 0)),
                     pl.BlockSpec((GRID_W, d // 2), lambda bi, i, j: (0, 0))]
        args += [per, pec]
    in_specs += [pl.BlockSpec((None, 1, d), lambda bi, i, j: (bi, 0, 0)),
                 pl.BlockSpec((None, 1, d), lambda bi, i, j: (bi, 0, 0)),
                 pl.BlockSpec((d, tn), lambda bi, i, j: (0, j)),
                 pl.BlockSpec((d, Z_PAD), lambda bi, i, j: (0, 0))]
    args += [gs, sh, w, wz]
    return pl.pallas_call(
        functools.partial(_inproj_kernel, tm=tm, with_pe=with_pe, n_main=n_main),
        grid=(b, l // tm, n // tn),
        in_specs=in_specs,
        out_specs=[pl.BlockSpec((None, tm, tn), lambda bi, i, j: (bi, i, jnp.minimum(j, n_main - 1))),
                   pl.BlockSpec((None, tm, tn), lambda bi, i, j: (bi, i, jnp.maximum(j - n_main, 0))),
                   pl.BlockSpec((None, tm, Z_PAD), lambda bi, i, j: (bi, i, 0))],
        out_shape=[jax.ShapeDtypeStruct((b, l, n_main_cols), BF16),
                   jax.ShapeDtypeStruct((b, l, n - n_main_cols), F32),
                   jax.ShapeDtypeStruct((b, l, Z_PAD), F32)],
        scratch_shapes=[pltpu.VMEM((tm, d), BF16)],
        compiler_params=_cparams(("parallel", "parallel", "arbitrary"), 56),
        name="input_projection_pe" if with_pe else "input_projection_ctx",
    )(*args)


def _tri_mask(rows, n, fwd):
    row = lax.broadcasted_iota(jnp.int32, (rows, rows), 0)
    col = lax.broadcasted_iota(jnp.int32, (rows, rows), 1)
    order = (col <= row) if fwd else (col >= row)
    if rows == n:
        return order
    return order & ((row // n) == (col // n))


def _gate_cum(z, gw_ref, gb_ref, mask):
    logit = _dot(z.astype(BF16), gw_ref[...]) + gb_ref[...]
    g = (jnp.minimum(logit, 0.0) - jnp.log1p(jnp.exp(-jnp.abs(logit)))) * (1.0 / GATE_NORM)
    tri = jnp.where(mask, 1.0, 0.0).astype(BF16)
    g_hi, g_lo = _split_bf16(g)
    return _dot(tri, g_hi) + _dot(tri, g_lo)


def _gla_kernel(*refs, fwd, n, nsub, nb):
    if fwd:
        (q_ref, k_ref, v_ref, z_ref, gw_ref, gb_ref, s0_ref, ob_ref, r_ref, ng_ref, o_ref, st_sc) = refs
    else:
        (q_ref, k_ref, v_ref, z_ref, gw_ref, gb_ref, s0_ref, o_ref, st_sc) = refs

    @pl.when(pl.program_id(0) == 0)
    def _():
        st_sc[...] = s0_ref[...]

    mask = _tri_mask(n, n, fwd)
    cmask = _tri_mask(nsub * n, n, fwd)
    last = n - 1 if fwd else 0
    scale = GLA_DK ** -0.5
    for bi in range(nb):
        b_all = _gate_cum(z_ref[bi], gw_ref, gb_ref, cmask)
        for cc in (range(nsub) if fwd else reversed(range(nsub))):
            rows = slice(cc * n, (cc + 1) * n)
            for h in range(GLA_HEADS):
                ks = slice(h * GLA_DK, (h + 1) * GLA_DK)
                vs = slice(h * GLA_DV, (h + 1) * GLA_DV)
                bh = b_all[rows, ks]
                bl = bh[last:last + 1, :]
                q = q_ref[bi, rows, ks].astype(F32) * scale
                k = k_ref[bi, rows, ks].astype(F32)
                v = v_ref[bi, rows, vs]
                qd = (q * jnp.exp(bh)).astype(BF16)
                ki = (k * jnp.exp(-bh)).astype(BF16)
                kt = (k * jnp.exp(bl - bh)).astype(BF16)
                att = jnp.where(mask, _dot_nt(qd, ki), 0.0).astype(BF16)
                st = st_sc[bi, h]
                o = _dot(att, v) + _dot_nt(qd, st.astype(BF16))
                st_sc[bi, h] = jnp.exp(bl) * st + _dot_tn(v, kt)
                if fwd:
                    o = o + ob_ref[bi, rows, vs]
                    on = o * lax.rsqrt(jnp.mean(o * o, axis=-1, keepdims=True) + EPS) * ng_ref[...]
                    o_ref[bi, rows, vs] = (on * _silu(r_ref[bi, rows, vs].astype(F32))).astype(BF16)
                else:
                    o_ref[bi, rows, vs] = o


def _gla_call(p, z, gw, gb, s0, fwd, ob=None, ng=None):
    b, l, _ = p.shape
    n = CHUNK
    nsub = GLA_SUBCHUNKS
    rows = n * nsub
    ns = l // rows
    hk = GLA_HEADS * GLA_DK
    hv = GLA_HEADS * GLA_DV
    cidx = (lambda c: c) if fwd else (lambda c: ns - 1 - c)
    in_specs = [pl.BlockSpec((b, rows, hk), lambda c: (0, cidx(c), 0)),
                pl.BlockSpec((b, rows, hk), lambda c: (0, cidx(c), 1)),
                pl.BlockSpec((b, rows, hv), lambda c: (0, cidx(c), 1)),
                pl.BlockSpec((b, rows, Z_PAD), lambda c: (0, cidx(c), 0)),
                pl.BlockSpec((Z_PAD, hk), lambda c: (0, 0)),
                pl.BlockSpec((1, hk), lambda c: (0, 0)),
                pl.BlockSpec((b, GLA_HEADS, GLA_DV, GLA_DK), lambda c: (0, 0, 0, 0))]
    args = [p, p, p, z, gw, gb, s0]
    if fwd:
        in_specs += [pl.BlockSpec((b, rows, hv), lambda c: (0, cidx(c), 0)),
                     pl.BlockSpec((b, rows, hv), lambda c: (0, cidx(c), 2)),
                     pl.BlockSpec((1, GLA_DV), lambda c: (0, 0))]
        args += [ob, p, ng]
    return pl.pallas_call(
        functools.partial(_gla_kernel, fwd=fwd, n=n, nsub=nsub, nb=b),
        grid=(ns,),
        in_specs=in_specs,
        out_specs=pl.BlockSpec((b, rows, hv), lambda c: (0, cidx(c), 0)),
        out_shape=jax.ShapeDtypeStruct((b, l, hv), BF16 if fwd else F32),
        scratch_shapes=[pltpu.VMEM((b, GLA_HEADS, GLA_DV, GLA_DK), F32)],
        compiler_params=_cparams(("arbitrary",), 32),
        name="gla_scan_fwd" if fwd else "gla_scan_bwd",
    )(*args)


def _ctx_state_kernel(k_ref, v_ref, z_ref, gw_ref, gb_ref, s_ref, *, fwd, n):
    mask = _tri_mask(n, n, fwd)
    b = _gate_cum(z_ref[...], gw_ref, gb_ref, mask)
    last = n - 1 if fwd else 0
    for h in range(GLA_HEADS):
        ks = slice(h * GLA_DK, (h + 1) * GLA_DK)
        vs = slice(h * GLA_DV, (h + 1) * GLA_DV)
        bh = b[:, ks]
        kt = (k_ref[:, ks].astype(F32) * jnp.exp(bh[last:last + 1, :] - bh)).astype(BF16)
        s_ref[h] = _dot_tn(v_ref[:, vs], kt)


def _ctx_state_call(pc, zc, gw, gb, fwd):
    b, n, _ = pc.shape
    hk = GLA_HEADS * GLA_DK
    hv = GLA_HEADS * GLA_DV
    return pl.pallas_call(
        functools.partial(_ctx_state_kernel, fwd=fwd, n=n),
        grid=(b,),
        in_specs=[pl.BlockSpec((None, n, hk), lambda bi: (bi, 0, 1)),
                  pl.BlockSpec((None, n, hv), lambda bi: (bi, 0, 1)),
                  pl.BlockSpec((None, n, Z_PAD), lambda bi: (bi, 0, 0)),
                  pl.BlockSpec((Z_PAD, hk), lambda bi: (0, 0)),
                  pl.BlockSpec((1, hk), lambda bi: (0, 0))],
        out_specs=pl.BlockSpec((None, GLA_HEADS, GLA_DV, GLA_DK), lambda bi: (bi, 0, 0, 0)),
        out_shape=jax.ShapeDtypeStruct((b, GLA_HEADS, GLA_DV, GLA_DK), F32),
        compiler_params=_cparams(("parallel",), 32),
        name="ctx_state_fwd" if fwd else "ctx_state_bwd",
    )(pc, pc, zc, gw, gb)


def _fft1_kernel(z_ref, w_ref, o_ref, *, n1, cw):
    for j in range(FFT_COLS):
        zj = z_ref[:, j, :]
        zz = jnp.concatenate([zj[:, :cw], zj[:, cw:]], axis=0).astype(BF16)
        a = _dot(w_ref[...], zz)
        o_ref[0, j] = a[:n1]
        o_ref[1, j] = a[n1:]


def _fft2_kernel(a_ref, m_ref, o_ref):
    for j in range(FFT_COLS):
        aa = jnp.concatenate([a_ref[0, :, j, :], a_ref[1, :, j, :]], axis=0).astype(BF16)
        o_ref[:, j, :] = _dot(m_ref[j], aa)


def _fourier_call(zf, w1, m2, n1, n2):
    b, l, cw2 = zf.shape
    cw = cw2 // 2
    fc = FFT_COLS
    a = pl.pallas_call(
        functools.partial(_fft1_kernel, n1=n1, cw=cw),
        grid=(b, n2 // fc),
        in_specs=[pl.BlockSpec((None, n1, fc, cw2), lambda bi, s: (bi, 0, s, 0)),
                  pl.BlockSpec((2 * n1, 2 * n1), lambda bi, s: (0, 0))],
        out_specs=pl.BlockSpec((None, 2, fc, n1, cw), lambda bi, s: (bi, 0, s, 0, 0)),
        out_shape=jax.ShapeDtypeStruct((b, 2, n2, n1, cw), F32),
        compiler_params=_cparams(("parallel", "parallel"), 48),
        name="position_dft_stage1",
    )(zf.reshape(b, n1, n2, cw2), w1)
    y = pl.pallas_call(
        _fft2_kernel,
        grid=(b, n1 // fc),
        in_specs=[pl.BlockSpec((None, 2, n2, fc, cw), lambda bi, s: (bi, 0, 0, s, 0)),
                  pl.BlockSpec((fc, n2, 2 * n2), lambda bi, s: (s, 0, 0))],
        out_specs=pl.BlockSpec((None, n2, fc, cw), lambda bi, s: (bi, 0, s, 0)),
        out_shape=jax.ShapeDtypeStruct((b, n2, n1, cw), F32),
        compiler_params=_cparams(("parallel", "parallel"), 48),
        name="position_dft_stage2",
    )(a, m2)
    return y.reshape(b, l, cw)


def _outproj_kernel(x1_ref, x2_ref, w_ref, xr_ref, per_ref, pec_ref, gate_ref, gsn_ref, shn_ref, o_ref, an_ref, *, tm):
    kh = x1_ref.shape[-1]
    half = o_ref.shape[-1] // 2
    y = _dot(x1_ref[...], w_ref[:kh, :]) + _dot(x2_ref[...].astype(BF16), w_ref[kh:, :])
    gate = gate_ref[...]
    for r in range(tm // ROWS):
        rows = slice(r * ROWS, (r + 1) * ROWS)
        pe = jnp.concatenate([jnp.broadcast_to(per_ref[r:r + 1, :], (ROWS, half)), pec_ref[...]], axis=-1)
        h = xr_ref[rows, :] + pe + gate * y[rows, :]
        o_ref[rows, :] = h
        an_ref[rows, :] = _rms_mod(h, gsn_ref[...], shn_ref[...])


def _outproj_call(x1, x2, w, x, per, pec, gate, gsn, shn, tm):
    b, l, d = x.shape
    kh = x1.shape[-1]
    half = d // 2
    bvec = pl.BlockSpec((None, 1, d), lambda bi, i: (bi, 0, 0))
    return pl.pallas_call(
        functools.partial(_outproj_kernel, tm=tm),
        grid=(b, l // tm),
        in_specs=[pl.BlockSpec((None, tm, kh), lambda bi, i: (bi, i, 0)),
                  pl.BlockSpec((None, tm, kh), lambda bi, i: (bi, i, 0)),
                  pl.BlockSpec((2 * kh, d), lambda bi, i: (0, 0), pipeline_mode=pl.Buffered(1)),
                  pl.BlockSpec((None, tm, d), lambda bi, i: (bi, i, 0)),
                  pl.BlockSpec((tm // GRID_W, half), lambda bi, i: (i, 0)),
                  pl.BlockSpec((GRID_W, half), lambda bi, i: (0, 0)),
                  bvec, bvec, bvec],
        out_specs=[pl.BlockSpec((None, tm, d), lambda bi, i: (bi, i, 0)),
                   pl.BlockSpec((None, tm, d), lambda bi, i: (bi, i, 0))],
        out_shape=[jax.ShapeDtypeStruct((b, l, d), F32), jax.ShapeDtypeStruct((b, l, d), BF16)],
        compiler_params=_cparams(("parallel", "parallel"), 56),
        name="output_projection",
    )(x1, x2, w, x, per, pec, gate, gsn, shn)


def _pair_kernel(*refs, glu):
    if glu:
        a_ref, wa_ref, wb_ref, ba_ref, bb_ref, o_ref = refs
    else:
        a_ref, wa_ref, wb_ref, o_ref = refs
    rows = a_ref.shape[0] // PAIR_SPLIT
    for s in range(PAIR_SPLIT):
        sl = slice(s * rows, (s + 1) * rows)
        a = a_ref[sl, :]
        ya = _dot(a, wa_ref[...])
        yb = _dot(a, wb_ref[...])
        if glu:
            o_ref[sl, :] = ((ya + ba_ref[...]) * jax.nn.sigmoid(yb + bb_ref[...])).astype(BF16)
        else:
            o_ref[sl, :] = (_silu(ya) * yb).astype(BF16)


def _pair_call(a, w, layer, bias, tm, tn, name):
    b, l, d = a.shape
    nh = w.shape[2] // 2
    nj = nh // tn
    glu = bias is not None
    in_specs = [pl.BlockSpec((None, tm, d), lambda bi, i, j: (bi, i, 0)),
                pl.BlockSpec((None, d, tn), lambda bi, i, j: (layer, 0, j)),
                pl.BlockSpec((None, d, tn), lambda bi, i, j: (layer, 0, j + nj))]
    args = [a, w, w]
    if glu:
        in_specs += [pl.BlockSpec((1, tn), lambda bi, i, j: (0, j)),
                     pl.BlockSpec((1, tn), lambda bi, i, j: (0, j + nj))]
        args += [bias, bias]
    return pl.pallas_call(
        functools.partial(_pair_kernel, glu=glu),
        grid=(b, l // tm, nj),
        in_specs=in_specs,
        out_specs=pl.BlockSpec((None, tm, tn), lambda bi, i, j: (bi, i, j)),
        out_shape=jax.ShapeDtypeStruct((b, l, nh), BF16),
        compiler_params=_cparams(("parallel", "parallel", "arbitrary"), 48),
        name=name,
    )(*args)


def _ffn_out_kernel(*refs, final):
    if final:
        x_ref, w_ref, res_ref, gate_ref, g_ref, o_ref = refs
    else:
        x_ref, w_ref, res_ref, gate_ref, gsn_ref, shn_ref, o_ref, an_ref = refs
    y = res_ref[...] + gate_ref[...] * _dot(x_ref[...], w_ref[...])
    if final:
        o_ref[...] = y * lax.rsqrt(jnp.mean(y * y, axis=-1, keepdims=True) + EPS) * g_ref[...]
    else:
        o_ref[...] = y
        an_ref[...] = _rms_mod(y, gsn_ref[...], shn_ref[...])


def _ffn_out_call(xh, w, layer, res, gate, final_g, nxt, tm):
    b, l, kk = xh.shape
    d = w.shape[2]
    final = final_g is not None
    row = pl.BlockSpec((None, tm, d), lambda bi, i: (bi, i, 0))
    bvec = pl.BlockSpec((None, 1, d), lambda bi, i: (bi, 0, 0))
    in_specs = [pl.BlockSpec((None, tm, kk), lambda bi, i: (bi, i, 0)),
                pl.BlockSpec((None, kk, d), lambda bi, i: (layer, 0, 0), pipeline_mode=pl.Buffered(1)),
                row, bvec]
    args = [xh, w, res, gate]
    if final:
        in_specs.append(pl.BlockSpec((1, d), lambda bi, i: (0, 0)))
        args.append(final_g)
        out_specs, out_shape = row, jax.ShapeDtypeStruct((b, l, d), F32)
    else:
        in_specs += [bvec, bvec]
        args += list(nxt)
        out_specs = [row, row]
        out_shape = [jax.ShapeDtypeStruct((b, l, d), F32), jax.ShapeDtypeStruct((b, l, d), BF16)]
    return pl.pallas_call(
        functools.partial(_ffn_out_kernel, final=final),
        grid=(b, l // tm),
        in_specs=in_specs,
        out_specs=out_specs,
        out_shape=out_shape,
        compiler_params=_cparams(("parallel", "parallel"), 56),
        name="ffn_output_final_norm" if final else "ffn_output_projection",
    )(*args)


def _conv_shift_matrix():
    rr = jnp.arange(7 * CONV_WIN)[:, None]
    return (jnp.arange(CONV_WIN + 8)[None, :] == (rr % CONV_WIN) + (rr // CONV_WIN) + 1).astype(BF16)


def _conv_ln_chunk(r, ext_sc, shm_ref, wdw_ref, bdw_ref, lng_ref, lnb_ref, cv_sc, a_dst):
    d = cv_sc.shape[-1]
    win = CONV_WIN
    lanes = 256
    r0 = pl.multiple_of(r * ROWS, ROWS)
    for cb in range(d // lanes):
        cs = slice(cb * lanes, (cb + 1) * lanes)
        e = ext_sc[pl.ds(r0, win + 8), cs]
        xs = _dot(shm_ref[...], e.astype(BF16))
        acc = None
        for s in range(8):
            for q in range(win // 8 - ROWS // 8 + 1):
                m = 8 * q + s
                if m == 0 or m > CONV_W:
                    continue
                lo = 8 * q if s == 0 else (s - 1) * win + 8 * q
                term = wdw_ref[m:m + 1, cs] * (e if s == 0 else xs)[lo:lo + ROWS, :]
                acc = term if acc is None else acc + term
        cv_sc[pl.ds(r0, ROWS), cs] = acc + bdw_ref[:, cs]
    c = cv_sc[pl.ds(r0, ROWS), :]
    xc = c - jnp.mean(c, axis=-1, keepdims=True)
    y = xc * lax.rsqrt(jnp.mean(xc * xc, axis=-1, keepdims=True) + EPS) * lng_ref[...] + lnb_ref[...]
    a_dst[pl.ds(r0, ROWS), :] = _silu(y).astype(BF16)


def _conv_pw2_kernel(um_ref, up_ref, un_ref, shm_ref, wdw_ref, bdw_ref, lng_ref, lnb_ref, w_ref, b2_ref, res_ref,
                     gate_ref, gsn_ref, shn_ref, o_ref, an_ref, ext_sc, cv_sc, a_sc, *, tm, nt):
    i = pl.program_id(1)
    ci = jnp.minimum(i, nt - 1)
    halo = CONV_HALO

    @pl.when(i == 0)
    def _():
        a_sc[1] = jnp.zeros(a_sc.shape[1:], BF16)

    ext_sc[0:halo, :] = jnp.where(ci > 0, up_ref[...].astype(F32), 0.0)
    ext_sc[halo:halo + tm, :] = um_ref[...].astype(F32)
    ext_sc[halo + tm:, :] = jnp.where(ci < nt - 1, un_ref[...].astype(F32), 0.0)
    cur = i % 2
    prev = 1 - cur
    n_slices = tm // ROWS
    cols = w_ref.shape[1] // n_slices

    def body(n, carry):
        c0 = pl.multiple_of(n * cols, cols)
        y = _dot(a_sc[prev], w_ref[:, pl.ds(c0, cols)]) + b2_ref[:, pl.ds(c0, cols)]
        o_ref[:, pl.ds(c0, cols)] = res_ref[:, pl.ds(c0, cols)] + gate_ref[:, pl.ds(c0, cols)] * y
        _conv_ln_chunk(n, ext_sc, shm_ref, wdw_ref, bdw_ref, lng_ref, lnb_ref, cv_sc, a_sc.at[cur])
        return carry

    lax.fori_loop(0, n_slices, body, 0)

    def act(r, carry):
        r0 = pl.multiple_of(r * ROWS, ROWS)
        an_ref[pl.ds(r0, ROWS), :] = _rms_mod(o_ref[pl.ds(r0, ROWS), :], gsn_ref[...], shn_ref[...])
        return carry

    lax.fori_loop(0, tm // ROWS, act, 0)


def _conv_pw2_call(u, wdw, bdw, lng, lnb, w, b2, res, gate, gsn, shn, tm):
    b, l, d = u.shape
    halo = CONV_HALO
    nt = l // tm
    hb = tm // halo
    nhb = l // halo
    shm = _conv_shift_matrix()

    def tile(i):
        return jnp.minimum(i, nt - 1)

    def prev(i):
        return jnp.maximum(i - 1, 0)

    vec = pl.BlockSpec((1, d), lambda bi, i: (0, 0))
    bvec = pl.BlockSpec((None, 1, d), lambda bi, i: (bi, 0, 0))
    return pl.pallas_call(
        functools.partial(_conv_pw2_kernel, tm=tm, nt=nt),
        grid=(b, nt + 1),
        in_specs=[pl.BlockSpec((None, tm, d), lambda bi, i: (bi, tile(i), 0)),
                  pl.BlockSpec((None, halo, d), lambda bi, i: (bi, jnp.maximum(tile(i) * hb - 1, 0), 0)),
                  pl.BlockSpec((None, halo, d), lambda bi, i: (bi, jnp.minimum((tile(i) + 1) * hb, nhb - 1), 0)),
                  pl.BlockSpec(shm.shape, lambda bi, i: (0, 0)),
                  pl.BlockSpec((CONV_W + 1, d), lambda bi, i: (0, 0)),
                  vec, vec, vec,
                  pl.BlockSpec((None, d, d), lambda bi, i: (0, 0, 0), pipeline_mode=pl.Buffered(1)),
                  vec,
                  pl.BlockSpec((None, tm, d), lambda bi, i: (bi, prev(i), 0)),
                  bvec, bvec, bvec],
        out_specs=[pl.BlockSpec((None, tm, d), lambda bi, i: (bi, prev(i), 0)),
                   pl.BlockSpec((None, tm, d), lambda bi, i: (bi, prev(i), 0))],
        out_shape=[jax.ShapeDtypeStruct((b, l, d), F32), jax.ShapeDtypeStruct((b, l, d), BF16)],
        scratch_shapes=[pltpu.VMEM((tm + 2 * halo, d), F32),
                        pltpu.VMEM((tm, d), F32),
                        pltpu.VMEM((2, tm, d), BF16)],
        compiler_params=_cparams(("parallel", "arbitrary"), 56),
        name="conv_ln_pointwise2",
    )(u, u, u, shm, wdw, bdw, lng, lnb, w, b2, res, gate, gsn, shn)


def _pos_tables(rows, d):
    nq = d // 4
    omega = 1.0 / (10000.0 ** (jnp.arange(nq, dtype=F32) / nq))
    ar = jnp.arange(rows, dtype=F32)[:, None] * omega[None, :]
    ac = jnp.arange(GRID_W, dtype=F32)[:, None] * omega[None, :]
    per = jnp.concatenate([jnp.sin(ar), jnp.cos(ar)], axis=-1)
    pec = jnp.concatenate([jnp.sin(ac), jnp.cos(ac)], axis=-1)
    return per, pec


def _cos_sin(num, den):
    ang = (num % den).astype(F32) * (2.0 * math.pi / den)
    return jnp.cos(ang), jnp.sin(ang)


def _dft_tables(n1, n2):
    i32 = jnp.int32
    gw = FN_GROUP_W
    cc, sc = _cos_sin(jnp.arange(gw, dtype=i32)[:, None] * jnp.arange(gw, dtype=i32)[None, :], gw)
    chan = jnp.concatenate([cc, -sc], axis=-1)
    c1, s1 = _cos_sin(jnp.arange(n1, dtype=i32)[:, None] * jnp.arange(n1, dtype=i32)[None, :], n1)
    w1 = jnp.concatenate([jnp.concatenate([c1, s1], axis=-1),
                          jnp.concatenate([-s1, c1], axis=-1)], axis=0).astype(BF16)
    cw_, sw_ = _cos_sin(jnp.arange(n2, dtype=i32)[:, None] * jnp.arange(n2, dtype=i32)[None, :], n2)
    ct_, st_ = _cos_sin(jnp.arange(n1, dtype=i32)[:, None] * jnp.arange(n2, dtype=i32)[None, :], n1 * n2)
    c2 = cw_[None, :, :] * ct_[:, None, :] - sw_[None, :, :] * st_[:, None, :]
    s2 = sw_[None, :, :] * ct_[:, None, :] + cw_[None, :, :] * st_[:, None, :]
    norm = 1.0 / math.sqrt(n1 * n2 * gw)
    m2 = (jnp.concatenate([c2, s2], axis=-1) * norm).astype(BF16)
    return chan, w1, m2


def _split_len(l):
    n1 = 1 << ((l.bit_length() - 1) // 2)
    n2 = l // n1
    assert n1 * n2 == l and n1 % 8 == 0 and n2 % 16 == 0, l
    return n1, n2


def kernel(x, c, ctx, c_ctx, ada_w, ada_b, norm_mix_g, norm_ffn_g, ffn_w_in, ffn_w_out, gm_w_in, gla_gate_w_fwd, gla_gate_b_fwd, gla_gate_w_bwd, gla_gate_b_bwd, gla_norm_g, gm_w_out, cv_w_pw1, cv_b_pw1, cv_w_dw, cv_b_dw, cv_ln_g, cv_ln_b, cv_w_pw2, cv_b_pw2, final_norm_g):
    b, l, d = x.shape
    depth = ada_w.shape[0]
    assert depth == 2 and b + 1 <= 8
    hk = GLA_HEADS * GLA_DK
    hv = GLA_HEADS * GLA_DV
    n1, n2 = _split_len(l)
    tm_huge = min(2048, l)
    tm_big = min(1024, l)
    tm_mid = min(512, l)
    tm_small = min(256, l)

    per, pec = _pos_tables(l // GRID_W, d)
    chan, w1, m2 = _dft_tables(n1, n2)

    cin = jnp.zeros((8, d), F32).at[:b].set(c).at[b].set(c_ctx)
    mod = _ada_call(cin, ada_w, ada_b).reshape(depth, 8, 6, d)

    def mod_vecs(layer, rows, g, which):
        shift, scale = mod[layer, rows, 3 * which], mod[layer, rows, 3 * which + 1]
        return (g[None, :] * (1.0 + scale))[:, None, :], shift[:, None, :]

    def gate_vec(layer, rows, which):
        return mod[layer, rows, 3 * which + 2][:, None, :]

    lat = slice(0, b)
    w_in = gm_w_in[0]
    q_off, k_off, v_off, r_off = 0, hk, 2 * hk, 2 * hk + hv
    zf_off = r_off + hv
    f_off = zf_off + 2 * GATE_RANK
    wzr, wzi = _fold_dft_call(w_in[:, f_off:], chan)
    w_main = jnp.concatenate([w_in[:, :zf_off].astype(BF16), wzr, wzi], axis=-1)
    w_z = jnp.pad(w_in[:, zf_off:f_off], ((0, 0), (0, Z_PAD - 2 * GATE_RANK))).astype(BF16)

    def gate_w(gw, off):
        return jnp.pad(gw, ((off, Z_PAD - GATE_RANK - off), (0, 0))).astype(BF16)

    gw_f, gw_b = gate_w(gla_gate_w_fwd[0], 0), gate_w(gla_gate_w_bwd[0], GATE_RANK)
    gb_f, gb_b = gla_gate_b_fwd[0][None, :], gla_gate_b_bwd[0][None, :]

    gs, sh = mod_vecs(0, lat, norm_mix_g[0], 0)
    gs_c, sh_c = mod_vecs(0, slice(b, b + 1), norm_mix_g[0], 0)
    gs_c, sh_c = jnp.broadcast_to(gs_c, (b, 1, d)), jnp.broadcast_to(sh_c, (b, 1, d))

    pc, _, zc = _inproj_call(ctx, None, gs_c, sh_c, w_main, w_z, ctx.shape[1], 512, zf_off)
    p, zf, z = _inproj_call(x, (per, pec), gs, sh, w_main, w_z, tm_big, 1024, zf_off)

    s0_f = _ctx_state_call(pc, zc, gw_f, gb_f, True)
    s0_b = _ctx_state_call(pc, zc, gw_b, gb_b, False)
    ob = _gla_call(p, z, gw_b, gb_b, s0_b, False)
    og = _gla_call(p, z, gw_f, gb_f, s0_f, True, ob=ob, ng=gla_norm_g[0][None, :])

    yf = _fourier_call(zf, w1, m2, n1, n2)

    w_ffn_in, w_ffn_out = ffn_w_in.astype(BF16), ffn_w_out.astype(BF16)
    h, a_ffn = _outproj_call(og, yf, gm_w_out[0].astype(BF16), x, per, pec, gate_vec(0, lat, 0),
                             *mod_vecs(0, lat, norm_ffn_g[0], 1), tm_mid)
    hid = _pair_call(a_ffn, w_ffn_in, 0, None, tm_huge, 512, "ffn_swiglu_in")
    h, a_mix = _ffn_out_call(hid, w_ffn_out, 0, h, gate_vec(0, lat, 1), None,
                             mod_vecs(1, lat, norm_mix_g[1], 0), tm_small)

    u = _pair_call(a_mix, cv_w_pw1.astype(BF16), 0, cv_b_pw1[0][None, :], tm_huge, 512, "pointwise1_glu")
    wdw = jnp.pad(cv_w_dw[0], ((1, 0), (0, 0)))
    h, a_ffn = _conv_pw2_call(u, wdw, cv_b_dw[0][None, :], cv_ln_g[0][None, :], cv_ln_b[0][None, :],
                              cv_w_pw2.astype(BF16), cv_b_pw2[0][None, :], h, gate_vec(1, lat, 0),
                              *mod_vecs(1, lat, norm_ffn_g[1], 1), tm_mid)
    hid = _pair_call(a_ffn, w_ffn_in, 1, None, tm_huge, 512, "ffn_swiglu_in")
    return _ffn_out_call(hid, w_ffn_out, 1, h, gate_vec(1, lat, 1), final_norm_g[None, :], None, tm_small)
```
